```python
import functools
import jax, jax.numpy as jnp
from jax import lax
import numpy as np

D_MODEL = 1024
BATCH = 32
SEQ = 2048
DEPTH = 1
DEC_BATCH = 128
DEC_SEQ = 8
PAST_LEN = 8192
PAGE_SIZE = 128

HEAD_DIM = 64
N_HEADS = D_MODEL // 2 // HEAD_DIM
ATTN_WIDTH = N_HEADS * HEAD_DIM
MOBA_BLOCK = 256
MOBA_TOPK = 3
Q_CHUNK = 128
ATTN_SCALE = HEAD_DIM ** -0.5
POOL_WINDOWS = (2, 4, 8, 16)
POOL_GROUPS = 4
POOL_WIDTH = D_MODEL // 2
POOL_GC = POOL_WIDTH // POOL_GROUPS
POOL_OUT = D_MODEL // POOL_GROUPS
POOL_PAST = max(POOL_WINDOWS) - 1
IN_WIDTH = 3 * ATTN_WIDTH + POOL_WIDTH + 2 * D_MODEL
SPLIT_POINTS = (ATTN_WIDTH, 2 * ATTN_WIDTH, 3 * ATTN_WIDTH, 3 * ATTN_WIDTH + POOL_WIDTH,
                3 * ATTN_WIDTH + POOL_WIDTH + D_MODEL)
N_EXPERTS = 32
TOP_K = 4
D_FF = D_MODEL
SWIGLU_LIMIT = 7.0
SWIGLU_ALPHA = 1.702
EXPERT_BLOCK = 128
N_MOD = 6
NORM_EPS = 1e-6

kernel_name = "moba_pool_moe_hybrid_step"


def rms_norm(x, g):
    xf = x.astype(jnp.float32)
    y = xf * lax.rsqrt(jnp.mean(xf * xf, axis=-1, keepdims=True) + NORM_EPS)
    return (y * g.astype(jnp.float32)).astype(x.dtype)


def moba_attend(q, k, v, q_pos):
    L = k.shape[0]
    Tq = q.shape[0]
    nb = -(-L // MOBA_BLOCK)
    pad = nb * MOBA_BLOCK - L
    kb = jnp.pad(k, ((0, pad), (0, 0), (0, 0))).reshape(nb, MOBA_BLOCK, N_HEADS, HEAD_DIM).transpose(2, 0, 1, 3)
    vb = jnp.pad(v, ((0, pad), (0, 0), (0, 0))).reshape(nb, MOBA_BLOCK, N_HEADS, HEAD_DIM).transpose(2, 0, 1, 3)
    kmean = jnp.mean(kb.astype(jnp.float32), axis=2)
    qblk = q_pos // MOBA_BLOCK
    gate = jnp.einsum('qhd,hnd->hqn', q.astype(jnp.float32), kmean)
    is_past = jnp.arange(nb)[None, None, :] < qblk[None, :, None]
    gate = jnp.where(is_past, gate, -jnp.inf)
    n_sel = min(MOBA_TOPK, nb)
    _, sel = lax.top_k(gate, n_sel)
    sel_ok = sel < qblk[None, :, None]
    own = jnp.broadcast_to(qblk[None, :, None], (N_HEADS, Tq, 1))
    blk = jnp.concatenate([sel, own], axis=-1)
    blk_ok = jnp.concatenate([sel_ok, jnp.ones(own.shape, dtype=bool)], axis=-1)
    hid = jnp.arange(N_HEADS)[:, None, None]
    kg = kb[hid, blk]
    vg = vb[hid, blk]
    kpos = blk[..., None] * MOBA_BLOCK + jnp.arange(MOBA_BLOCK)
    mask = blk_ok[..., None] & (kpos <= q_pos[None, :, None, None])
    s = jnp.einsum('qhd,hqjkd->hqjk', q, kg, preferred_element_type=jnp.float32) * ATTN_SCALE
    s = jnp.where(mask, s, -jnp.inf).reshape(N_HEADS, Tq, -1)
    pr = jax.nn.softmax(s, axis=-1).reshape(mask.shape).astype(v.dtype)
    return jnp.einsum('hqjk,hqjkd->qhd', pr, vg)


def moba_prompt(q, k, v):
    B, S = q.shape[:2]
    n_chunks = S // Q_CHUNK
    qc = q.reshape(B * n_chunks, Q_CHUNK, N_HEADS, HEAD_DIM)
    ids = jnp.arange(B * n_chunks)

    def one(args):
        qi, i = args
        b = i // n_chunks
        pos = (i % n_chunks) * Q_CHUNK + jnp.arange(Q_CHUNK)
        return moba_attend(qi, k[b], v[b], pos)

    return lax.map(one, (qc, ids)).reshape(B, S, N_HEADS, HEAD_DIM)


def moba_sample(q, k, v, cache_k, cache_v, page_table, layer):
    T = q.shape[1]
    q_pos = PAST_LEN + jnp.arange(T)

    def one(args):
        qi, ki, vi, pages = args
        pk = cache_k[layer, pages].reshape(-1, N_HEADS, HEAD_DIM)
        pv = cache_v[layer, pages].reshape(-1, N_HEADS, HEAD_DIM)
        return moba_attend(qi, jnp.concatenate([pk, ki], axis=0), jnp.concatenate([pv, vi], axis=0), q_pos)

    return lax.map(one, (q, k, v, page_table))


def multiscale_pool(pz, prefix, pos0):
    B, T, C = pz.shape
    xp = jnp.concatenate([prefix, pz], axis=1).astype(jnp.float32)
    cs = jnp.concatenate([jnp.zeros((B, 1, C), jnp.float32), jnp.cumsum(xp, axis=1)], axis=1)
    end = cs[:, POOL_PAST + 1:]
    pos = pos0 + jnp.arange(T)
    outs = []
    for g, w in enumerate(POOL_WINDOWS):
        sl = slice(g * POOL_GC, (g + 1) * POOL_GC)
        start = cs[:, POOL_PAST + 1 - w: POOL_PAST + 1 - w + T, sl]
        cnt = jnp.minimum(w, pos + 1).astype(jnp.float32)[None, :, None]
        outs.append((end[..., sl] - start) / cnt)
    mean = jnp.concatenate(outs, axis=-1)
    return (mean - pz.astype(jnp.float32)).astype(pz.dtype)


def moe_ffn(xt, w_router, b_router, w_gu, b_gu, w_down, b_down):
    T, D = xt.shape
    TK = T * TOP_K
    logits = jnp.dot(xt, w_router, preferred_element_type=jnp.float32) + b_router.astype(jnp.float32)
    top_val, top_idx = lax.top_k(logits, TOP_K)
    gates = jax.nn.softmax(top_val, axis=-1)
    e_flat = top_idx.reshape(-1)
    tok_flat = jnp.broadcast_to(jnp.arange(T)[:, None], (T, TOP_K)).reshape(-1)
    g_flat = gates.reshape(-1)
    order = jnp.argsort(e_flat)
    e_s, tok_s, g_s = e_flat[order], tok_flat[order], g_flat[order]
    counts = jnp.bincount(e_flat, length=N_EXPERTS)
    starts = jnp.cumsum(counts) - counts
    padded = (counts + EXPERT_BLOCK - 1) // EXPERT_BLOCK * EXPERT_BLOCK
    pends = jnp.cumsum(padded)
    pstarts = pends - padded
    dest = pstarts[e_s] + (jnp.arange(TK) - starts[e_s])
    n_blocks = (TK + N_EXPERTS * (EXPERT_BLOCK - 1) + EXPERT_BLOCK - 1) // EXPERT_BLOCK
    P = n_blocks * EXPERT_BLOCK
    xs = jnp.zeros((P, D), xt.dtype).at[dest].set(xt[tok_s])
    blk_e = jnp.minimum(jnp.searchsorted(pends, jnp.arange(n_blocks) * EXPERT_BLOCK, side='right'), N_EXPERTS - 1)

    def expert_block(args):
        xb, e = args
        gu = xb @ w_gu[e] + b_gu[e]
        gt = jnp.minimum(gu[:, :D_FF], SWIGLU_LIMIT)
        up = jnp.clip(gu[:, D_FF:], -SWIGLU_LIMIT, SWIGLU_LIMIT)
        glu = gt * jax.nn.sigmoid(gt * SWIGLU_ALPHA)
        return ((up + 1) * glu) @ w_down[e] + b_down[e]

    ys = lax.map(expert_block, (xs.reshape(n_blocks, EXPERT_BLOCK, D), blk_e)).reshape(P, D)
    contrib = ys[dest] * g_s[:, None].astype(ys.dtype)
    return jax.ops.segment_sum(contrib, tok_s, num_segments=T)


def decoder_layer(x, c, attend, pool_prefix, pos0, p):
    B, T, _ = x.shape
    mod = jax.nn.silu(c) @ p['w_mod'] + p['b_mod']
    sh1, sc1, gt1, sh2, sc2, gt2 = jnp.split(mod[:, None, :], N_MOD, axis=-1)
    h = rms_norm(x, p['g_pre1']) * (1 + sc1) + sh1
    u = h @ p['w_in']
    q, k, v, pz, ga, gb = jnp.split(u, SPLIT_POINTS, axis=-1)
    q = q.reshape(B, T, N_HEADS, HEAD_DIM)
    k = k.reshape(B, T, N_HEADS, HEAD_DIM)
    v = v.reshape(B, T, N_HEADS, HEAD_DIM)
    att = attend(q, k, v).reshape(B, T, ATTN_WIDTH) @ p['w_attn_o']
    pooled = multiscale_pool(pz, pool_prefix, pos0)
    pb = jnp.einsum('btgc,gce->btge', pooled.reshape(B, T, POOL_GROUPS, POOL_GC), p['w_pool'])
    pb = pb.reshape(B, T, D_MODEL) * p['pool_scale']
    mix = (jax.nn.sigmoid(ga) * att + jax.nn.sigmoid(gb) * pb) @ p['w_out']
    x = x + gt1 * rms_norm(mix, p['g_post1'])
    h2 = rms_norm(x, p['g_pre2']) * (1 + sc2) + sh2
    ff = moe_ffn(h2.reshape(B * T, D_MODEL), p['w_router'], p['b_router'], p['w_gu'], p['b_gu'],
                 p['w_down'], p['b_down']).reshape(B, T, D_MODEL)
    x = x + gt2 * rms_norm(ff, p['g_post2'])
    pool_rows = jnp.concatenate([pool_prefix, pz], axis=1)[:, -POOL_PAST:]
    return x, k, v, pool_rows


def setup_inputs(seed: int = 0) -> dict:
    key = jax.random.key(seed)
    ks = jax.random.split(key, 26)
    n_pages = PAST_LEN // PAGE_SIZE
    n_phys = (DEC_BATCH * n_pages * 5) // 4
    nrm = lambda k_, shp, s: jax.random.normal(k_, shp, jnp.float32) * s
    page_table = jax.random.permutation(ks[7], n_phys)[: DEC_BATCH * n_pages].reshape(DEC_BATCH, n_pages).astype(jnp.int32)
    return {
        'x_prompt': nrm(ks[0], (BATCH, SEQ, D_MODEL), 1.0),
        'x_sample': nrm(ks[1], (DEC_BATCH, DEC_SEQ, D_MODEL), 1.0),
        'c_prompt': nrm(ks[2], (BATCH, D_MODEL), 1.0),
        'c_sample': nrm(ks[3], (DEC_BATCH, D_MODEL), 1.0),
        'cache_k': nrm(ks[4], (DEPTH, n_phys, PAGE_SIZE, N_HEADS, HEAD_DIM), 1.0),
        'cache_v': nrm(ks[5], (DEPTH, n_phys, PAGE_SIZE, N_HEADS, HEAD_DIM), 1.0),
        'state_pool': nrm(ks[6], (DEPTH, DEC_BATCH, POOL_PAST, POOL_WIDTH), 1.0),
        'page_table': page_table,
        'w_mod': nrm(ks[8], (DEPTH, D_MODEL, N_MOD * D_MODEL), 0.5 * D_MODEL ** -0.5),
        'b_mod': nrm(ks[9], (DEPTH, N_MOD * D_MODEL), 0.01),
        'g_pre1': 1.0 + nrm(ks[10], (DEPTH, D_MODEL), 0.05),
        'g_post1': 1.0 + nrm(ks[11], (DEPTH, D_MODEL), 0.05),
        'g_pre2': 1.0 + nrm(ks[12], (DEPTH, D_MODEL), 0.05),
        'g_post2': 1.0 + nrm(ks[13], (DEPTH, D_MODEL), 0.05),
        'w_in': nrm(ks[14], (DEPTH, D_MODEL, IN_WIDTH), D_MODEL ** -0.5),
        'w_attn_o': nrm(ks[15], (DEPTH, ATTN_WIDTH, D_MODEL), ATTN_WIDTH ** -0.5),
        'w_pool': nrm(ks[16], (DEPTH, POOL_GROUPS, POOL_GC, POOL_OUT), POOL_GC ** -0.5),
        'pool_scale': 1.0 + nrm(ks[17], (DEPTH, D_MODEL), 0.1),
        'w_out': nrm(ks[18], (DEPTH, D_MODEL, D_MODEL), D_MODEL ** -0.5),
        'w_router': nrm(ks[19], (DEPTH, D_MODEL, N_EXPERTS), D_MODEL ** -0.5),
        'b_router': nrm(ks[20], (DEPTH, N_EXPERTS), 0.01),
        'w_gu': nrm(ks[21], (DEPTH, N_EXPERTS, D_MODEL, 2 * D_FF), D_MODEL ** -0.5),
        'b_gu': nrm(ks[22], (DEPTH, N_EXPERTS, 2 * D_FF), 0.01),
        'w_down': nrm(ks[23], (DEPTH, N_EXPERTS, D_FF, D_MODEL), D_FF ** -0.5),
        'b_down': nrm(ks[24], (DEPTH, N_EXPERTS, D_MODEL), 0.01),
    }


def reference(x_prompt, x_sample, c_prompt, c_sample, cache_k, cache_v, state_pool, page_table,
              w_mod, b_mod, g_pre1, g_post1, g_pre2, g_post2, w_in, w_attn_o, w_pool, pool_scale, w_out,
              w_router, b_router, w_gu, b_gu, w_down, b_down):
    h_prompt, h_sample = x_prompt, x_sample
    kp_l, vp_l, pp_l, ks_l, vs_l, ps_l = [], [], [], [], [], []
    for l in range(DEPTH):
        p = {'w_mod': w_mod[l], 'b_mod': b_mod[l], 'g_pre1': g_pre1[l], 'g_post1': g_post1[l],
             'g_pre2': g_pre2[l], 'g_post2': g_post2[l], 'w_in': w_in[l], 'w_attn_o': w_attn_o[l],
             'w_pool': w_pool[l], 'pool_scale': pool_scale[l], 'w_out': w_out[l], 'w_router': w_router[l],
             'b_router': b_router[l], 'w_gu': w_gu[l], 'b_gu': b_gu[l], 'w_down': w_down[l], 'b_down': b_down[l]}
        zero_prefix = jnp.zeros((h_prompt.shape[0], POOL_PAST, POOL_WIDTH), h_prompt.dtype)
        h_prompt, kp, vp, pp = decoder_layer(h_prompt, c_prompt, moba_prompt, zero_prefix, 0, p)
        attend_s = functools.partial(moba_sample, cache_k=cache_k, cache_v=cache_v, page_table=page_table, layer=l)
        h_sample, k_s, v_s, p_s = decoder_layer(h_sample, c_sample, attend_s, state_pool[l], PAST_LEN, p)
        kp_l.append(kp); vp_l.append(vp); pp_l.append(pp)
        ks_l.append(k_s); vs_l.append(v_s); ps_l.append(p_s)
    k_prompt = jnp.stack(kp_l)
    v_prompt = jnp.stack(vp_l)
    pool_prompt = jnp.stack(pp_l)
    k_sample = jnp.stack(ks_l)
    v_sample = jnp.stack(vs_l)
    pool_sample = jnp.stack(ps_l)
    return (h_prompt, h_sample, k_prompt, v_prompt, pool_prompt, k_sample, v_sample, pool_sample)
```

```python
import functools

import jax
import jax.numpy as jnp
from jax import lax
from jax.experimental import pallas as pl
from jax.experimental.pallas import tpu as pltpu

F32 = jnp.float32
BF16 = jnp.bfloat16
HIGHEST = lax.Precision.HIGHEST

MOBA_BLOCK = 256
MOBA_TOPK = 3
POOL_WINDOWS = (2, 4, 8, 16)
POOL_PREFIX_ROWS = 16
MOE_TOP_K = 4
SWIGLU_LIMIT = 7.0
SWIGLU_ALPHA = 1.702
N_MOD = 6
NORM_EPS = 1e-6
MASK_BIAS = -1e30

VMEM_LIMIT = 56 * 1024 * 1024
PROMPT_ROWS = 512
SAMPLE_SEQS = 16
PAGES_PER_STEP = 8
PLAN_ROWS = 512
DISPATCH_ROWS = 512
EXPERT_ROWS = 512
COMBINE_ROWS = 256


def _cparams(sem):
    return pltpu.CompilerParams(dimension_semantics=sem, vmem_limit_bytes=VMEM_LIMIT)


def _rms(x, g):
    return x * lax.rsqrt(jnp.mean(x * x, axis=-1, keepdims=True) + NORM_EPS) * g


def _dot_nt(a, b, precision=None):
    return lax.dot_general(a, b, (((1,), (1,)), ((), ())), preferred_element_type=F32, precision=precision)


def _mod_kernel(c_ref, w_ref, b_ref, o_ref):
    c = c_ref[...]
    s = c * jax.nn.sigmoid(c)
    o_ref[...] = jnp.dot(s, w_ref[...], preferred_element_type=F32, precision=HIGHEST) + b_ref[...]


def _mod(c_all, w_mod, b_mod):
    n, d = c_all.shape
    return pl.pallas_call(
        _mod_kernel,
        grid=(N_MOD,),
        in_specs=[pl.BlockSpec((n, d), lambda j: (0, 0)),
                  pl.BlockSpec((d, d), lambda j: (0, j)),
                  pl.BlockSpec((1, d), lambda j: (0, j))],
        out_specs=pl.BlockSpec((n, d), lambda j: (0, j)),
        out_shape=jax.ShapeDtypeStruct((n, N_MOD * d), F32),
        compiler_params=_cparams(("arbitrary",)),
        name="mod",
    )(c_all, w_mod, b_mod)


def _in_proj_kernel(x_ref, sh_ref, sc_ref, g_ref, w_ref, q_ref, k_ref, v_ref, pz_ref, gab_ref, *, aw, pw):
    bb, tt, d = x_ref.shape
    h = _rms(x_ref[...], g_ref[...]) * (1.0 + sc_ref[...]) + sh_ref[...]
    hb = h.reshape(bb * tt, d).astype(BF16)
    col = 0
    for ref, width in ((q_ref, aw), (k_ref, aw), (v_ref, aw), (pz_ref, pw), (gab_ref, 2 * d)):
        u = jnp.dot(hb, w_ref[:, col:col + width], preferred_element_type=F32)
        ref[...] = u.reshape(bb, tt, width)
        col += width


def _in_proj(x, mod3, g_pre1, w_in_bf, aw, pw, bb, tt):
    bx, t, d = x.shape
    grid = (bx // bb, t // tt)
    row = lambda width: pl.BlockSpec((bb, tt, width), lambda b, i: (b, i, 0))
    modspec = lambda slot: pl.BlockSpec((bb, 1, d), lambda b, i: (b, 0, slot))
    out = lambda width: jax.ShapeDtypeStruct((bx, t, width), F32)
    return pl.pallas_call(
        functools.partial(_in_proj_kernel, aw=aw, pw=pw),
        grid=grid,
        in_specs=[row(d), modspec(0), modspec(1),
                  pl.BlockSpec((1, d), lambda b, i: (0, 0)),
                  pl.BlockSpec(w_in_bf.shape, lambda b, i: (0, 0))],
        out_specs=[row(aw), row(aw), row(aw), row(pw), row(2 * d)],
        out_shape=[out(aw), out(aw), out(aw), out(pw), out(2 * d)],
        compiler_params=_cparams(("arbitrary", "arbitrary")),
        name="in_proj",
    )(x, mod3, mod3, g_pre1, w_in_bf)


def _moba_select(gate, n_past, n_cand):
    slot = lax.broadcasted_iota(jnp.int32, gate.shape, 1)
    past = slot < n_past
    g = jnp.where(past, gate, -jnp.inf)
    rank = jnp.zeros(gate.shape, jnp.int32)
    for j in range(n_cand):
        col = g[:, j:j + 1]
        beats = (col > g) | ((col == g) & (j < slot))
        rank = rank + jnp.where(beats & (j < n_past), 1, 0)
    return past & (rank < MOBA_TOPK)


def _prompt_attn_kernel(q_ref, k_ref, v_ref, o_ref, kaug_ref, vbf_ref, kmean_ref, *, n_heads, hd, scale):
    i = pl.program_id(1)
    s_len = k_ref.shape[1]
    tq = q_ref.shape[1]
    nb = s_len // MOBA_BLOCK
    slots = kmean_ref.shape[1]

    @pl.when(i == 0)
    def _prepare():
        kf = k_ref[0]
        vbf_ref[...] = v_ref[0].astype(BF16)
        key_blk = lax.broadcasted_iota(jnp.int32, (s_len, slots), 0) // MOBA_BLOCK
        onehot = jnp.where(key_blk == lax.broadcasted_iota(jnp.int32, (s_len, slots), 1), 1.0, 0.0).astype(BF16)
        ksum = kf.reshape(nb, MOBA_BLOCK, kf.shape[1]).sum(axis=1) * (1.0 / MOBA_BLOCK)
        kmean_ref[...] = jnp.zeros(kmean_ref.shape, F32)
        for h in range(n_heads):
            kaug_ref[h, :, 0:hd] = kf[:, h * hd:(h + 1) * hd].astype(BF16)
            kaug_ref[h, :, hd:hd + slots] = onehot
            kmean_ref[h, 0:nb, :] = ksum[:, h * hd:(h + 1) * hd]

    qf = q_ref[0]
    slot = lax.broadcasted_iota(jnp.int32, (tq, slots), 1)
    row_id = lax.broadcasted_iota(jnp.int32, (tq, MOBA_BLOCK), 0)
    col_id = lax.broadcasted_iota(jnp.int32, (tq, MOBA_BLOCK), 1)
    diag0 = pl.multiple_of(i * MOBA_BLOCK, MOBA_BLOCK)
    outs = []
    for h in range(n_heads):
        qh = qf[:, h * hd:(h + 1) * hd]
        gate = _dot_nt(qh, kmean_ref[h], precision=HIGHEST)
        sel = _moba_select(gate, i, nb) | (slot == i)
        bias = jnp.where(sel, 0.0, MASK_BIAS)
        qaug = jnp.concatenate([qh * scale, bias], axis=1).astype(BF16)

        s = _dot_nt(qaug, kaug_ref[h, pl.ds(diag0, MOBA_BLOCK), :])
        s = jnp.where(col_id <= row_id, s, -jnp.inf)
        m = jnp.max(s, axis=-1, keepdims=True)
        p = jnp.exp(s - m)
        l = jnp.sum(p, axis=-1, keepdims=True)
        acc = jnp.dot(p.astype(BF16), vbf_ref[pl.ds(diag0, MOBA_BLOCK), h * hd:(h + 1) * hd],
                      preferred_element_type=F32)

        def body(j, carry, qaug=qaug, h=h):
            m, l, acc = carry
            r0 = pl.multiple_of(j * MOBA_BLOCK, MOBA_BLOCK)
            s = _dot_nt(qaug, kaug_ref[h, pl.ds(r0, MOBA_BLOCK), :])
            m_new = jnp.maximum(m, jnp.max(s, axis=-1, keepdims=True))
            alpha = jnp.exp(m - m_new)
            p = jnp.exp(s - m_new)
            l = alpha * l + jnp.sum(p, axis=-1, keepdims=True)
            acc = alpha * acc + jnp.dot(p.astype(BF16), vbf_ref[pl.ds(r0, MOBA_BLOCK), h * hd:(h + 1) * hd],
                                        preferred_element_type=F32)
            return m_new, l, acc

        m, l, acc = lax.fori_loop(0, i, body, (m, l, acc))
        outs.append(acc / l)
    o_ref[0] = jnp.concatenate(outs, axis=1)


def _prompt_attention(q, k, v, n_heads, hd):
    b, s_len, aw = q.shape
    assert s_len % MOBA_BLOCK == 0
    nb = s_len // MOBA_BLOCK
    slots = hd
    assert nb <= slots
    tq = MOBA_BLOCK
    return pl.pallas_call(
        functools.partial(_prompt_attn_kernel, n_heads=n_heads, hd=hd, scale=hd ** -0.5),
        grid=(b, s_len // tq),
        in_specs=[pl.BlockSpec((1, tq, aw), lambda bi, i: (bi, i, 0)),
                  pl.BlockSpec((1, s_len, aw), lambda bi, i: (bi, 0, 0)),
                  pl.BlockSpec((1, s_len, aw), lambda bi, i: (bi, 0, 0))],
        out_specs=pl.BlockSpec((1, tq, aw), lambda bi, i: (bi, i, 0)),
        out_shape=jax.ShapeDtypeStruct((b, s_len, aw), F32),
        scratch_shapes=[pltpu.VMEM((n_heads, s_len, hd + slots), BF16),
                        pltpu.VMEM((s_len, aw), BF16),
                        pltpu.VMEM((n_heads, slots, hd), F32)],
        compiler_params=_cparams(("arbitrary", "arbitrary")),
        name="prompt_attention",
    )(q, k, v)


def _sample_attn_kernel(pt_ref, qt_ref, kn_ref, vn_ref, *rest, n_heads, hd, t_new, page, scale):
    del pt_ref
    npg = PAGES_PER_STEP
    k_pages = rest[:npg]
    v_pages = rest[npg:2 * npg]
    o_ref = rest[2 * npg]
    m_ref, l_ref, acc_ref, ksum_ref = rest[2 * npg + 1:]
    step = pl.program_id(1)
    n_steps = pl.num_programs(1)
    rows = n_heads * t_new
    pages_per_blk = MOBA_BLOCK // page
    blks_per_step = npg // pages_per_blk
    n_past = m_ref.shape[0]
    keys = MOBA_BLOCK * n_heads

    q_all = (qt_ref[0].reshape(rows, hd) * scale).astype(BF16)
    row_head = lax.broadcasted_iota(jnp.int32, (rows, keys), 0) // t_new
    key_head = lax.broadcasted_iota(jnp.int32, (rows, keys), 1) % n_heads
    valid = row_head == key_head

    for bi in range(blks_per_step):
        kparts, vparts, ksum = [], [], None
        for r in range(bi * pages_per_blk, (bi + 1) * pages_per_blk):
            kp = k_pages[r][0, 0]
            kparts.append(kp.reshape(page * n_heads, hd).astype(BF16))
            vparts.append(v_pages[r][0, 0].reshape(page * n_heads, hd).astype(BF16))
            psum = kp.sum(axis=0)
            ksum = psum if ksum is None else ksum + psum
        kb = jnp.concatenate(kparts, axis=0)
        vb = jnp.concatenate(vparts, axis=0)
        s = jnp.where(valid, _dot_nt(q_all, kb), -jnp.inf)
        m = jnp.max(s, axis=-1, keepdims=True)
        p = jnp.exp(s - m)
        l = jnp.sum(p, axis=-1, keepdims=True)
        acc = jnp.dot(p.astype(BF16), vb, preferred_element_type=F32)
        j = step * blks_per_step + bi
        m_ref[j] = jnp.broadcast_to(m, (rows, 128))
        l_ref[j] = jnp.broadcast_to(l, (rows, 128))
        acc_ref[j] = acc
        for h in range(n_heads):
            ksum_ref[h, pl.ds(j, 1), :] = ksum[h:h + 1, :]

    @pl.when(step == n_steps - 1)
    def _finish():
        kn = kn_ref[0].reshape(t_new * n_heads, hd).astype(BF16)
        vn = vn_ref[0].reshape(t_new * n_heads, hd).astype(BF16)
        nk = t_new * n_heads
        r_id = lax.broadcasted_iota(jnp.int32, (rows, nk), 0)
        c_id = lax.broadcasted_iota(jnp.int32, (rows, nk), 1)
        own_ok = ((r_id // t_new) == (c_id % n_heads)) & ((c_id // n_heads) <= (r_id % t_new))
        s = jnp.where(own_ok, _dot_nt(q_all, kn), -jnp.inf)
        m_tot = jnp.max(s, axis=-1, keepdims=True)
        p = jnp.exp(s - m_tot)
        den = jnp.sum(p, axis=-1, keepdims=True)
        num = jnp.dot(p.astype(BF16), vn, preferred_element_type=F32)

        sels = []
        for h in range(n_heads):
            gate = _dot_nt(qt_ref[0, h], ksum_ref[h] * (1.0 / MOBA_BLOCK), precision=HIGHEST)
            sels.append(_moba_select(gate, n_past, n_past))
        sel = jnp.concatenate(sels, axis=0)

        m_sel = m_tot
        for j in range(n_past):
            m_sel = jnp.maximum(m_sel, jnp.where(sel[:, j:j + 1], m_ref[j][:, 0:1], -jnp.inf))
        w_own = jnp.exp(m_tot - m_sel)
        num = num * w_own
        den = den * w_own
        for j in range(n_past):
            w = jnp.where(sel[:, j:j + 1], jnp.exp(m_ref[j][:, 0:1] - m_sel), 0.0)
            num = num + w * acc_ref[j]
            den = den + w * l_ref[j][:, 0:1]
        o_ref[0] = (num / den).reshape(n_heads, t_new, hd)


def _sample_attention(qt, k_new, v_new, cache_k, cache_v, page_table, layer):
    n, n_heads, t_new, hd = qt.shape
    page = cache_k.shape[2]
    n_pages = page_table.shape[1]
    npg = PAGES_PER_STEP
    assert n_pages % npg == 0 and MOBA_BLOCK % page == 0 and npg % (MOBA_BLOCK // page) == 0
    n_past = n_pages * page // MOBA_BLOCK
    rows = n_heads * t_new

    def page_spec(r):
        return pl.BlockSpec((1, 1, page, n_heads, hd),
                            lambda ni, s, pt: (layer, pt[ni, s * npg + r], 0, 0, 0))

    new_spec = pl.BlockSpec((1, t_new, n_heads, hd), lambda ni, s, pt: (ni, 0, 0, 0))
    qt_spec = pl.BlockSpec((1, n_heads, t_new, hd), lambda ni, s, pt: (ni, 0, 0, 0))
    grid_spec = pltpu.PrefetchScalarGridSpec(
        num_scalar_prefetch=1,
        grid=(n, n_pages // npg),
        in_specs=[qt_spec, new_spec, new_spec] + [page_spec(r) for r in range(npg)] * 2,
        out_specs=qt_spec,
        scratch_shapes=[pltpu.VMEM((n_past, rows, 128), F32),
                        pltpu.VMEM((n_past, rows, 128), F32),
                        pltpu.VMEM((n_past, rows, hd), F32),
                        pltpu.VMEM((n_heads, n_past, hd), F32)],
    )
    return pl.pallas_call(
        functools.partial(_sample_attn_kernel, n_heads=n_heads, hd=hd, t_new=t_new, page=page, scale=hd ** -0.5),
        grid_spec=grid_spec,
        out_shape=jax.ShapeDtypeStruct(qt.shape, F32),
        compiler_params=_cparams(("arbitrary", "arbitrary")),
        name="sample_attention",
    )(page_table, qt, k_new, v_new, *([cache_k] * npg), *([cache_v] * npg))


def _post_kernel(att_ref, pz_ref, pre_ref, gab_ref, x_ref, gt1_ref, sh2_ref, sc2_ref,
                 wao_ref, wpool_ref, pscale_ref, wout_ref, gpost1_ref, gpre2_ref, wr_ref, br_ref,
                 x1_ref, h2_ref, topi_ref, gates_ref, *, pos0, zero_first_prefix):
    bb, tt, d = x_ref.shape
    rows = bb * tt
    i = pl.program_id(1)
    n_groups = len(POOL_WINDOWS)
    gc = pz_ref.shape[2] // n_groups

    pz = pz_ref[...]
    prefix = pre_ref[...]
    if zero_first_prefix:
        prefix = jnp.where(i == 0, 0.0, prefix)
    xp = jnp.concatenate([prefix, pz], axis=1)
    pos = pos0 + i * tt + lax.broadcasted_iota(jnp.int32, (1, tt, 1), 1)
    pbs = []
    for g, w in enumerate(POOL_WINDOWS):
        a = xp[:, :, g * gc:(g + 1) * gc]
        span = 1
        while span < w:
            a = a + pltpu.roll(a, span, 1)
            span *= 2
        win = a[:, POOL_PREFIX_ROWS:, :]
        cnt = jnp.minimum(w, pos + 1).astype(F32)
        pooled = win / cnt - pz[:, :, g * gc:(g + 1) * gc]
        pbs.append(jnp.dot(pooled.reshape(rows, gc).astype(BF16), wpool_ref[g], preferred_element_type=F32))
    pb = jnp.concatenate(pbs, axis=1) * pscale_ref[...]

    att = jnp.dot(att_ref[...].reshape(rows, att_ref.shape[2]).astype(BF16), wao_ref[...],
                  preferred_element_type=F32)
    gab = gab_ref[...].reshape(rows, 2 * d)
    mix = jax.nn.sigmoid(gab[:, :d]) * att + jax.nn.sigmoid(gab[:, d:]) * pb
    mix = jnp.dot(mix.astype(BF16), wout_ref[...], preferred_element_type=F32)
    x1 = x_ref[...] + gt1_ref[...] * _rms(mix, gpost1_ref[...]).reshape(bb, tt, d)
    x1_ref[...] = x1
    h2 = (_rms(x1, gpre2_ref[...]) * (1.0 + sc2_ref[...]) + sh2_ref[...]).reshape(rows, d)
    h2_ref[...] = h2.reshape(bb, tt, d)

    logits = jnp.dot(h2, wr_ref[...], preferred_element_type=F32, precision=HIGHEST) + br_ref[...]
    n_exp = logits.shape[1]
    lane = lax.broadcasted_iota(jnp.int32, logits.shape, 1).astype(F32)
    kcol = lax.broadcasted_iota(jnp.int32, (rows, MOE_TOP_K), 1)
    vals = jnp.zeros((rows, MOE_TOP_K), F32)
    idxs = jnp.zeros((rows, MOE_TOP_K), F32)
    v = logits
    for k in range(MOE_TOP_K):
        mx = jnp.max(v, axis=-1, keepdims=True)
        ix = jnp.min(jnp.where(v == mx, lane, float(n_exp)), axis=-1, keepdims=True)
        vals = jnp.where(kcol == k, mx, vals)
        idxs = jnp.where(kcol == k, ix, idxs)
        v = jnp.where(lane == ix, -jnp.inf, v)
    e = jnp.exp(vals - vals[:, 0:1])
    gates = e / jnp.sum(e, axis=-1, keepdims=True)
    topi_ref[...] = idxs.astype(jnp.int32).reshape(bb, tt, MOE_TOP_K)
    gates_ref[...] = gates.reshape(bb, tt, MOE_TOP_K)


def _post(att, pz, prefix, gab, x, mod3, weights, bb, tt, pos0, zero_first_prefix):
    bx, t, d = x.shape
    aw, pw = att.shape[2], pz.shape[2]
    grid = (bx // bb, t // tt)
    row = lambda width: pl.BlockSpec((bb, tt, width), lambda b, i: (b, i, 0))
    modspec = lambda slot: pl.BlockSpec((bb, 1, d), lambda b, i: (b, 0, slot))
    full = lambda a: pl.BlockSpec(a.shape, lambda b, i: (0,) * a.ndim)
    if zero_first_prefix:
        per_tile = tt // POOL_PREFIX_ROWS
        pre_spec = pl.BlockSpec((bb, POOL_PREFIX_ROWS, pw), lambda b, i: (b, jnp.maximum(i * per_tile - 1, 0), 0))
    else:
        assert t == tt
        pre_spec = pl.BlockSpec((bb, POOL_PREFIX_ROWS, pw), lambda b, i: (b, 0, 0))
    out = lambda width, dt: jax.ShapeDtypeStruct((bx, t, width), dt)
    return pl.pallas_call(
        functools.partial(_post_kernel, pos0=pos0, zero_first_prefix=zero_first_prefix),
        grid=grid,
        in_specs=[row(aw), row(pw), pre_spec, row(2 * d), row(d), modspec(2), modspec(3), modspec(4)]
                 + [full(w) for w in weights],
        out_specs=[row(d), row(d), row(MOE_TOP_K), row(MOE_TOP_K)],
        out_shape=[out(d, F32), out(d, F32), out(MOE_TOP_K, jnp.int32), out(MOE_TOP_K, F32)],
        compiler_params=_cparams(("arbitrary", "arbitrary")),
        name="post",
    )(att, pz, prefix, gab, x, mod3, mod3, mod3, *weights)


def _plan_kernel(topi_ref, rank_ref, counts_ref, base_ref):
    i = pl.program_id(0)
    tt = topi_ref.shape[0]
    n_exp = counts_ref.shape[1]

    @pl.when(i == 0)
    def _init():
        base_ref[...] = jnp.zeros(base_ref.shape, F32)

    topi = topi_ref[...]
    lane = lax.broadcasted_iota(jnp.int32, (tt, n_exp), 1)
    onehots = [jnp.where(topi[:, k:k + 1] == lane, 1.0, 0.0) for k in range(MOE_TOP_K)]
    tot = onehots[0]
    for oh in onehots[1:]:
        tot = tot + oh
    tri = jnp.where(lax.broadcasted_iota(jnp.int32, (tt, tt), 0) > lax.broadcasted_iota(jnp.int32, (tt, tt), 1),
                    1.0, 0.0).astype(BF16)
    before = jnp.dot(tri, tot.astype(BF16), preferred_element_type=F32) + base_ref[...]
    kcol = lax.broadcasted_iota(jnp.int32, (tt, MOE_TOP_K), 1)
    rank = jnp.zeros((tt, MOE_TOP_K), F32)
    for k in range(MOE_TOP_K):
        rank = jnp.where(kcol == k, jnp.sum(onehots[k] * before, axis=-1, keepdims=True), rank)
    rank_ref[...] = rank.astype(jnp.int32)
    base_ref[...] = base_ref[...] + jnp.sum(tot, axis=0, keepdims=True)
    counts_ref[...] = base_ref[...].astype(jnp.int32)


def _moe_plan(topi, n_exp):
    t = topi.shape[0]
    tt = PLAN_ROWS
    assert t % tt == 0
    return pl.pallas_call(
        _plan_kernel,
        grid=(t // tt,),
        in_specs=[pl.BlockSpec((tt, MOE_TOP_K), lambda i: (i, 0))],
        out_specs=[pl.BlockSpec((tt, MOE_TOP_K), lambda i: (i, 0)),
                   pl.BlockSpec((1, n_exp), lambda i: (0, 0))],
        out_shape=[jax.ShapeDtypeStruct((t, MOE_TOP_K), jnp.int32),
                   jax.ShapeDtypeStruct((1, n_exp), jnp.int32)],
        scratch_shapes=[pltpu.VMEM((1, n_exp), F32)],
        compiler_params=_cparams(("arbitrary",)),
        name="moe_plan",
    )(topi)


def _dispatch_kernel(dest_ref, h_ref, xs_in_ref, xs_ref, sem):
    del xs_in_ref
    tt = h_ref.shape[0]

    def row_copy(r, k):
        return pltpu.make_async_copy(h_ref.at[pl.ds(r, 1), :],
                                     xs_ref.at[pl.ds(dest_ref[0, 0, r * MOE_TOP_K + k], 1), :], sem)

    def issue(r, carry):
        for k in range(MOE_TOP_K):
            row_copy(r, k).start()
        return carry

    lax.fori_loop(0, tt, issue, 0)
    for k in range(MOE_TOP_K):
        pltpu.make_async_copy(h_ref, xs_ref.at[pl.ds(0, tt), :], sem).wait()


def _moe_dispatch(dest3, h2_all, xs_init):
    t, d = h2_all.shape
    tt = DISPATCH_ROWS
    assert t % tt == 0
    return pl.pallas_call(
        _dispatch_kernel,
        grid=(t // tt,),
        in_specs=[pl.BlockSpec((1, 1, tt * MOE_TOP_K), lambda i: (i, 0, 0), memory_space=pltpu.SMEM),
                  pl.BlockSpec((tt, d), lambda i: (i, 0)),
                  pl.BlockSpec(memory_space=pl.ANY)],
        out_specs=pl.BlockSpec(memory_space=pl.ANY),
        out_shape=jax.ShapeDtypeStruct(xs_init.shape, F32),
        scratch_shapes=[pltpu.SemaphoreType.DMA(())],
        input_output_aliases={2: 0},
        compiler_params=_cparams(("arbitrary",)),
        name="moe_dispatch",
    )(dest3, h2_all, xs_init)


def _expert_kernel(blk_e_ref, n_used_ref, xs_ref, wgu_ref, bgu_ref, wd_ref, bd_ref, ys_ref):
    del blk_e_ref
    i = pl.program_id(0)
    dff = wd_ref.shape[1]

    @pl.when(i < n_used_ref[0])
    def _compute():
        x = xs_ref[...].astype(BF16)
        gu = jnp.dot(x, wgu_ref[0], preferred_element_type=F32) + bgu_ref[0]
        gt = jnp.minimum(gu[:, :dff], SWIGLU_LIMIT)
        up = jnp.clip(gu[:, dff:], -SWIGLU_LIMIT, SWIGLU_LIMIT)
        act = (up + 1.0) * (gt * jax.nn.sigmoid(gt * SWIGLU_ALPHA))
        ys_ref[...] = jnp.dot(act.astype(BF16), wd_ref[0], preferred_element_type=F32) + bd_ref[0]

    @pl.when(i >= n_used_ref[0])
    def _skip():
        ys_ref[...] = jnp.zeros(ys_ref.shape, F32)


def _moe_experts(blk_e, n_used, xs, w_gu_bf, b_gu3, w_down_bf, b_down3):
    p, d = xs.shape
    bm = EXPERT_ROWS
    n_exp, _, two_f = w_gu_bf.shape
    dff = w_down_bf.shape[1]
    grid_spec = pltpu.PrefetchScalarGridSpec(
        num_scalar_prefetch=2,
        grid=(p // bm,),
        in_specs=[pl.BlockSpec((bm, d), lambda i, be, nu: (i, 0)),
                  pl.BlockSpec((1, d, two_f), lambda i, be, nu: (be[i], 0, 0)),
                  pl.BlockSpec((1, 1, two_f), lambda i, be, nu: (be[i], 0, 0)),
                  pl.BlockSpec((1, dff, d), lambda i, be, nu: (be[i], 0, 0)),
                  pl.BlockSpec((1, 1, d), lambda i, be, nu: (be[i], 0, 0))],
        out_specs=pl.BlockSpec((bm, d), lambda i, be, nu: (i, 0)),
    )
    return pl.pallas_call(
        _expert_kernel,
        grid_spec=grid_spec,
        out_shape=jax.ShapeDtypeStruct((p, d), F32),
        compiler_params=_cparams(("arbitrary",)),
        name="moe_experts",
    )(blk_e, n_used, xs, w_gu_bf, b_gu3, w_down_bf, b_down3)


def _combine_kernel(dest_ref, gates_ref, x1_ref, gt2_ref, gpost2_ref, ys_ref, y_ref, buf_ref, sem):
    bb, tt, d = x1_ref.shape
    rows = bb * tt

    def row_copy(r, k):
        return pltpu.make_async_copy(ys_ref.at[pl.ds(dest_ref[0, 0, r * MOE_TOP_K + k], 1), :],
                                     buf_ref.at[k, pl.ds(r, 1), :], sem)

    def issue(r, carry):
        for k in range(MOE_TOP_K):
            row_copy(r, k).start()
        return carry

    lax.fori_loop(0, rows, issue, 0)
    for k in range(MOE_TOP_K):
        pltpu.make_async_copy(ys_ref.at[pl.ds(0, rows), :], buf_ref.at[k], sem).wait()

    gates = gates_ref[...].reshape(rows, MOE_TOP_K)
    ff = gates[:, 0:1] * buf_ref[0]
    for k in range(1, MOE_TOP_K):
        ff = ff + gates[:, k:k + 1] * buf_ref[k]
    y_ref[...] = x1_ref[...] + gt2_ref[...] * _rms(ff, gpost2_ref[...]).reshape(bb, tt, d)


def _moe_combine(dest3, gates, x1, mod3, g_post2, ys, bb, tt):
    bx, t, d = x1.shape
    rows = bb * tt
    n_t = t // tt
    row = lambda width: pl.BlockSpec((bb, tt, width), lambda b, i: (b, i, 0))
    return pl.pallas_call(
        _combine_kernel,
        grid=(bx // bb, n_t),
        in_specs=[pl.BlockSpec((1, 1, rows * MOE_TOP_K), lambda b, i: (b * n_t + i, 0, 0), memory_space=pltpu.SMEM),
                  row(MOE_TOP_K), row(d),
                  pl.BlockSpec((bb, 1, d), lambda b, i: (b, 0, 5)),
                  pl.BlockSpec((1, d), lambda b, i: (0, 0)),
                  pl.BlockSpec(memory_space=pl.ANY)],
        out_specs=row(d),
        out_shape=jax.ShapeDtypeStruct((bx, t, d), F32),
        scratch_shapes=[pltpu.VMEM((MOE_TOP_K, rows, d), F32), pltpu.SemaphoreType.DMA(())],
        compiler_params=_cparams(("arbitrary", "arbitrary")),
        name="moe_combine",
    )(dest3, gates, x1, mod3, g_post2, ys)


def _layer(xp, xs, cp, cs, cache_k, cache_v, state_pool_l, page_table, layer, p):
    b, s_len, d = xp.shape
    n, t_new, _ = xs.shape
    n_heads, hd = cache_k.shape[3], cache_k.shape[4]
    aw = n_heads * hd
    pw = state_pool_l.shape[2]
    n_exp = p['w_router'].shape[1]
    sb = min(SAMPLE_SEQS, n)
    pr = min(PROMPT_ROWS, s_len)

    mod = _mod(jnp.concatenate([cp, cs], axis=0), p['w_mod'], p['b_mod'])
    mod_p = mod[:b].reshape(b, 1, N_MOD * d)
    mod_s = mod[b:].reshape(n, 1, N_MOD * d)

    w_in_bf = p['w_in'].astype(BF16)
    qp, kp, vp, pzp, gabp = _in_proj(xp, mod_p, p['g_pre1'], w_in_bf, aw, pw, 1, pr)
    qs, ks, vs, pzs, gabs = _in_proj(xs, mod_s, p['g_pre1'], w_in_bf, aw, pw, sb, t_new)

    att_p = _prompt_attention(qp, kp, vp, n_heads, hd)
    ks4 = ks.reshape(n, t_new, n_heads, hd)
    vs4 = vs.reshape(n, t_new, n_heads, hd)
    qt = qs.reshape(n, t_new, n_heads, hd).transpose(0, 2, 1, 3)
    att_s = _sample_attention(qt, ks4, vs4, cache_k, cache_v, page_table, layer)
    att_s = att_s.transpose(0, 2, 1, 3).reshape(n, t_new, aw)

    post_w = (p['w_attn_o'].astype(BF16), p['w_pool'].astype(BF16), p['pool_scale'], p['w_out'].astype(BF16),
              p['g_post1'], p['g_pre2'], p['w_router'], p['b_router'])
    x1p, h2p, tip, gp = _post(att_p, pzp, pzp, gabp, xp, mod_p, post_w, 1, pr, 0, True)
    state16 = jnp.concatenate([jnp.zeros((n, 1, pw), F32), state_pool_l], axis=1)
    past_len = page_table.shape[1] * cache_k.shape[2]
    x1s, h2s, tis, gs = _post(att_s, pzs, state16, gabs, xs, mod_s, post_w, sb, t_new, past_len, False)

    tp, ts = b * s_len, n * t_new
    h2_all = jnp.concatenate([h2p.reshape(tp, d), h2s.reshape(ts, d)], axis=0)
    topi = jnp.concatenate([tip.reshape(tp, MOE_TOP_K), tis.reshape(ts, MOE_TOP_K)], axis=0)
    t_all = tp + ts
    rank, counts = _moe_plan(topi, n_exp)
    bm = EXPERT_ROWS
    counts = counts[0]
    padded = (counts + bm - 1) // bm * bm
    pends = jnp.cumsum(padded)
    pstarts = pends - padded
    n_blocks = (t_all * MOE_TOP_K + n_exp * (bm - 1) + bm - 1) // bm
    blk_e = jnp.minimum(jnp.searchsorted(pends, jnp.arange(n_blocks, dtype=jnp.int32) * bm, side='right'),
                        n_exp - 1).astype(jnp.int32)
    n_used = (pends[-1:] // bm).astype(jnp.int32)
    dest = pstarts[topi].astype(jnp.int32) + rank

    xs_sorted = _moe_dispatch(dest.reshape(t_all // DISPATCH_ROWS, 1, DISPATCH_ROWS * MOE_TOP_K), h2_all,
                              jnp.zeros((n_blocks * bm, d), F32))
    ys = _moe_experts(blk_e, n_used, xs_sorted, p['w_gu'].astype(BF16), p['b_gu'][:, None, :],
                      p['w_down'].astype(BF16), p['b_down'][:, None, :])

    cr = min(COMBINE_ROWS, s_len)
    csb = min(max(COMBINE_ROWS // t_new, 1), n)
    dest_p = dest[:tp].reshape(tp // cr, 1, cr * MOE_TOP_K)
    dest_s = dest[tp:].reshape(ts // (csb * t_new), 1, csb * t_new * MOE_TOP_K)
    yp = _moe_combine(dest_p, gp, x1p, mod_p, p['g_post2'], ys, 1, cr)
    ysm = _moe_combine(dest_s, gs, x1s, mod_s, p['g_post2'], ys, csb, t_new)

    pool_p = pzp[:, s_len - (POOL_PREFIX_ROWS - 1):, :]
    pool_s = jnp.concatenate([state_pool_l, pzs], axis=1)[:, -(POOL_PREFIX_ROWS - 1):, :]
    return (yp, ysm, kp.reshape(b, s_len, n_heads, hd), vp.reshape(b, s_len, n_heads, hd), pool_p,
            ks4, vs4, pool_s)


def kernel(x_prompt, x_sample, c_prompt, c_sample, cache_k, cache_v, state_pool, page_table, w_mod, b_mod, g_pre1, g_post1, g_pre2, g_post2, w_in, w_attn_o, w_pool, pool_scale, w_out, w_router, b_router, w_gu, b_gu, w_down, b_down):
    depth = w_mod.shape[0]
    hp, hs = x_prompt, x_sample
    outs = [[] for _ in range(6)]
    for l in range(depth):
        p = {'w_mod': w_mod[l], 'b_mod': b_mod[l][None, :], 'g_pre1': g_pre1[l][None, :],
             'g_post1': g_post1[l][None, :], 'g_pre2': g_pre2[l][None, :], 'g_post2': g_post2[l][None, :],
             'w_in': w_in[l], 'w_attn_o': w_attn_o[l], 'w_pool': w_pool[l], 'pool_scale': pool_scale[l][None, :],
             'w_out': w_out[l], 'w_router': w_router[l], 'b_router': b_router[l][None, :],
             'w_gu': w_gu[l], 'b_gu': b_gu[l], 'w_down': w_down[l], 'b_down': b_down[l]}
        hp, hs, kp, vp, pp, ks, vs, ps = _layer(hp, hs, c_prompt, c_sample, cache_k, cache_v, state_pool[l],
                                                page_table, l, p)
        for lst, val in zip(outs, (kp, vp, pp, ks, vs, ps)):
            lst.append(val)
    return (hp, hs) + tuple(jnp.stack(o) for o in outs)
```

```python
import functools

import jax
import jax.numpy as jnp
from jax import lax
from jax.experimental import pallas as pl
from jax.experimental.pallas import tpu as pltpu

F32 = jnp.float32
BF16 = jnp.bfloat16
HIGHEST = lax.Precision.HIGHEST

MOBA_BLOCK = 256
MOBA_TOPK = 3
POOL_WINDOWS = (2, 4, 8, 16)
POOL_PREFIX_ROWS = 16
MOE_TOP_K = 4
SWIGLU_LIMIT = 7.0
SWIGLU_ALPHA = 1.702
N_MOD = 6
NORM_EPS = 1e-6
MASK_BIAS = -1e30

LANES = 128
SUBLANES = 8
VMEM_LIMIT = 56 * 1024 * 1024
PROMPT_ROWS = 512
SAMPLE_SEQS = 16
PAGES_PER_STEP = 8
PLAN_ROWS = 512
DISPATCH_ROWS = 512
EXPERT_ROWS = 512
COMBINE_ROWS = 256


def _cparams(sem):
    return pltpu.CompilerParams(dimension_semantics=sem, vmem_limit_bytes=VMEM_LIMIT)


def _rms(x, g):
    return x * lax.rsqrt(jnp.mean(x * x, axis=-1, keepdims=True) + NORM_EPS) * g


def _dot_nt(a, b, precision=None):
    return lax.dot_general(a, b, (((1,), (1,)), ((), ())), preferred_element_type=F32, precision=precision)


def _mod_kernel(c_ref, w_ref, b_ref, o_ref):
    c = c_ref[...]
    s = c * jax.nn.sigmoid(c)
    o_ref[...] = jnp.dot(s, w_ref[...], preferred_element_type=F32, precision=HIGHEST) + b_ref[...]


def _mod(c_all, w_mod, b_mod):
    n, d = c_all.shape
    return pl.pallas_call(
        _mod_kernel,
        grid=(N_MOD,),
        in_specs=[pl.BlockSpec((n, d), lambda j: (0, 0)),
                  pl.BlockSpec((d, d), lambda j: (0, j)),
                  pl.BlockSpec((1, d), lambda j: (0, j))],
        out_specs=pl.BlockSpec((n, d), lambda j: (0, j)),
        out_shape=jax.ShapeDtypeStruct((n, N_MOD * d), F32),
        compiler_params=_cparams(("arbitrary",)),
        name="mod",
    )(c_all, w_mod, b_mod)


def _modulated(x_ref, sh_ref, sc_ref, g_ref):
    bb, tt, d = x_ref.shape
    h = _rms(x_ref[...], g_ref[...]) * (1.0 + sc_ref[...]) + sh_ref[...]
    return h.reshape(bb * tt, d).astype(BF16)


def _in_proj_sample_kernel(x_ref, sh_ref, sc_ref, g_ref, w_ref, q_ref, k_ref, v_ref, pz_ref, gab_ref, *, aw, pw):
    bb, tt, d = x_ref.shape
    hb = _modulated(x_ref, sh_ref, sc_ref, g_ref)
    col = 0
    for ref, width in ((q_ref, aw), (k_ref, aw), (v_ref, aw), (pz_ref, pw), (gab_ref, 2 * d)):
        u = jnp.dot(hb, w_ref[:, col:col + width], preferred_element_type=F32)
        ref[...] = u.reshape(bb, tt, width)
        col += width


def _in_proj_prompt_kernel(x_ref, sh_ref, sc_ref, g_ref, w_ref, wkvt_ref,
                           q_ref, k_ref, kt_ref, vt_ref, vtb_ref, pz_ref, gab_ref, *, aw, pw):
    _, tt, d = x_ref.shape
    hb = _modulated(x_ref, sh_ref, sc_ref, g_ref)
    q_ref[0] = jnp.dot(hb, w_ref[:, 0:aw], preferred_element_type=F32).astype(BF16)
    k_ref[0] = jnp.dot(hb, w_ref[:, aw:2 * aw], preferred_element_type=F32).astype(BF16)
    col = 3 * aw
    pz_ref[0] = jnp.dot(hb, w_ref[:, col:col + pw], preferred_element_type=F32)
    gab_ref[0] = jnp.dot(hb, w_ref[:, col + pw:col + pw + 2 * d], preferred_element_type=F32)
    kvt = _dot_nt(wkvt_ref[...], hb)
    kt_ref[0] = kvt[0:aw]
    vt_ref[0] = kvt[aw:2 * aw]
    for j in range(tt // MOBA_BLOCK):
        vtb_ref[0, j] = kvt[aw:2 * aw, j * MOBA_BLOCK:(j + 1) * MOBA_BLOCK].astype(BF16)


def _in_proj_sample(x, mod3, g_pre1, w_in_bf, aw, pw, bb):
    bx, t, d = x.shape
    row = lambda width: pl.BlockSpec((bb, t, width), lambda b: (b, 0, 0))
    modspec = lambda slot: pl.BlockSpec((bb, 1, d), lambda b: (b, 0, slot))
    out = lambda width: jax.ShapeDtypeStruct((bx, t, width), F32)
    return pl.pallas_call(
        functools.partial(_in_proj_sample_kernel, aw=aw, pw=pw),
        grid=(bx // bb,),
        in_specs=[row(d), modspec(0), modspec(1),
                  pl.BlockSpec((1, d), lambda b: (0, 0)),
                  pl.BlockSpec(w_in_bf.shape, lambda b: (0, 0))],
        out_specs=[row(aw), row(aw), row(aw), row(pw), row(2 * d)],
        out_shape=[out(aw), out(aw), out(aw), out(pw), out(2 * d)],
        compiler_params=_cparams(("arbitrary",)),
        name="in_proj_sample",
    )(x, mod3, mod3, g_pre1, w_in_bf)


def _in_proj_prompt(x, mod3, g_pre1, w_in_bf, w_kvt_bf, aw, pw, tt):
    bx, t, d = x.shape
    assert tt % MOBA_BLOCK == 0 and t % tt == 0
    nb, per = t // MOBA_BLOCK, tt // MOBA_BLOCK
    row = lambda width: pl.BlockSpec((1, tt, width), lambda b, i: (b, i, 0))
    col = pl.BlockSpec((1, aw, tt), lambda b, i: (b, 0, i))
    modspec = lambda slot: pl.BlockSpec((1, 1, d), lambda b, i: (b, 0, slot))
    return pl.pallas_call(
        functools.partial(_in_proj_prompt_kernel, aw=aw, pw=pw),
        grid=(bx, t // tt),
        in_specs=[row(d), modspec(0), modspec(1),
                  pl.BlockSpec((1, d), lambda b, i: (0, 0)),
                  pl.BlockSpec(w_in_bf.shape, lambda b, i: (0, 0)),
                  pl.BlockSpec(w_kvt_bf.shape, lambda b, i: (0, 0))],
        out_specs=[row(aw), row(aw), col, col,
                   pl.BlockSpec((1, per, aw, MOBA_BLOCK), lambda b, i: (b, i, 0, 0)),
                   row(pw), row(2 * d)],
        out_shape=[jax.ShapeDtypeStruct((bx, t, aw), BF16), jax.ShapeDtypeStruct((bx, t, aw), BF16),
                   jax.ShapeDtypeStruct((bx, aw, t), F32), jax.ShapeDtypeStruct((bx, aw, t), F32),
                   jax.ShapeDtypeStruct((bx, nb, aw, MOBA_BLOCK), BF16),
                   jax.ShapeDtypeStruct((bx, t, pw), F32), jax.ShapeDtypeStruct((bx, t, 2 * d), F32)],
        compiler_params=_cparams(("arbitrary", "arbitrary")),
        name="in_proj_prompt",
    )(x, mod3, mod3, g_pre1, w_in_bf, w_kvt_bf)


def _moba_select(gate, n_past, n_cand, axis):
    slot = lax.broadcasted_iota(jnp.int32, gate.shape, axis)
    past = slot < n_past
    g = jnp.where(past, gate, -jnp.inf)
    rank = jnp.zeros(gate.shape, jnp.int32)
    for j in range(n_cand):
        other = g[:, j:j + 1] if axis == 1 else g[j:j + 1, :]
        beats = (other > g) | ((other == g) & (j < slot))
        rank = rank + jnp.where(beats & (j < n_past), 1, 0)
    return past & (rank < MOBA_TOPK)


def _prompt_attn_kernel(q_ref, k_ref, vtb_ref, o_ref, kaug_ref, kmean_ref, qaug_ref, *, n_heads, hd, scale):
    i = pl.program_id(1)
    s_len = k_ref.shape[1]
    tq = q_ref.shape[1]
    nb = s_len // MOBA_BLOCK
    nbp = kmean_ref.shape[1]
    slots = kaug_ref.shape[2] - hd

    @pl.when(i == 0)
    def _prepare():
        kf = k_ref[0].astype(F32)
        key_blk = lax.broadcasted_iota(jnp.int32, (s_len, slots), 0) // MOBA_BLOCK
        onehot = jnp.where(key_blk == lax.broadcasted_iota(jnp.int32, (s_len, slots), 1), 1.0, 0.0).astype(BF16)
        kmean = kf.reshape(nb, MOBA_BLOCK, kf.shape[1]).sum(axis=1) * (1.0 / MOBA_BLOCK)
        kmean_ref[...] = jnp.zeros(kmean_ref.shape, F32)
        for h in range(n_heads):
            kaug_ref[h, :, 0:hd] = k_ref[0, :, h * hd:(h + 1) * hd]
            kaug_ref[h, :, hd:hd + slots] = onehot
            kmean_ref[h, 0:nb, :] = kmean[:, h * hd:(h + 1) * hd]

    qt = q_ref[0].astype(F32).T
    slot = lax.broadcasted_iota(jnp.int32, (nbp, tq), 0)
    gates = [jnp.dot(kmean_ref[h], qt[h * hd:(h + 1) * hd], preferred_element_type=F32, precision=HIGHEST)
             for h in range(n_heads)]
    for h in range(n_heads):
        sel = _moba_select(gates[h], i, nb, axis=0) | (slot == i)
        bias = jnp.where(sel, 0.0, MASK_BIAS)
        parts = [qt[h * hd:(h + 1) * hd] * scale, bias]
        if slots > nbp:
            parts.append(jnp.zeros((slots - nbp, tq), F32))
        qaug_ref[h] = jnp.concatenate(parts, axis=0).astype(BF16)

    def scores(h, r0):
        return jnp.dot(kaug_ref[h, pl.ds(r0, MOBA_BLOCK), :], qaug_ref[h], preferred_element_type=F32)

    key_id = lax.broadcasted_iota(jnp.int32, (MOBA_BLOCK, tq), 0)
    qry_id = lax.broadcasted_iota(jnp.int32, (MOBA_BLOCK, tq), 1)
    diag0 = pl.multiple_of(i * MOBA_BLOCK, MOBA_BLOCK)
    ss = [scores(h, diag0) for h in range(n_heads)]
    soft = []
    for h in range(n_heads):
        s = jnp.where(key_id <= qry_id, ss[h], -jnp.inf)
        m = jnp.max(s, axis=0, keepdims=True)
        p = jnp.exp(s - m)
        soft.append((m, jnp.sum(p, axis=0, keepdims=True), p.astype(BF16)))
    state = [(m, l, jnp.dot(vtb_ref[0, i, h * hd:(h + 1) * hd, :], p, preferred_element_type=F32))
             for h, (m, l, p) in enumerate(soft)]

    def body(j, state):
        r0 = pl.multiple_of(j * MOBA_BLOCK, MOBA_BLOCK)
        ss = [scores(h, r0) for h in range(n_heads)]
        soft = []
        for h in range(n_heads):
            m_old, l_old, _ = state[h]
            m_new = jnp.maximum(m_old, jnp.max(ss[h], axis=0, keepdims=True))
            alpha = jnp.exp(m_old - m_new)
            p = jnp.exp(ss[h] - m_new)
            soft.append((m_new, alpha, alpha * l_old + jnp.sum(p, axis=0, keepdims=True), p.astype(BF16)))
        new_state = []
        for h in range(n_heads):
            m_new, alpha, l_new, p = soft[h]
            acc = alpha * state[h][2] + jnp.dot(vtb_ref[0, j, h * hd:(h + 1) * hd, :], p,
                                                preferred_element_type=F32)
            new_state.append((m_new, l_new, acc))
        return tuple(new_state)

    state = lax.fori_loop(0, i, body, tuple(state))
    o_ref[0] = jnp.concatenate([acc / l for _, l, acc in state], axis=0).T


def _prompt_attention(q_bf, k_bf, vtb, n_heads, hd):
    b, s_len, aw = q_bf.shape
    assert s_len % MOBA_BLOCK == 0
    nb = s_len // MOBA_BLOCK
    nbp = -(-nb // SUBLANES) * SUBLANES
    slots = LANES - hd
    assert nbp <= slots
    tq = MOBA_BLOCK
    return pl.pallas_call(
        functools.partial(_prompt_attn_kernel, n_heads=n_heads, hd=hd, scale=hd ** -0.5),
        grid=(b, s_len // tq),
        in_specs=[pl.BlockSpec((1, tq, aw), lambda bi, i: (bi, i, 0)),
                  pl.BlockSpec((1, s_len, aw), lambda bi, i: (bi, 0, 0)),
                  pl.BlockSpec((1, nb, aw, MOBA_BLOCK), lambda bi, i: (bi, 0, 0, 0))],
        out_specs=pl.BlockSpec((1, tq, aw), lambda bi, i: (bi, i, 0)),
        out_shape=jax.ShapeDtypeStruct((b, s_len, aw), F32),
        scratch_shapes=[pltpu.VMEM((n_heads, s_len, hd + slots), BF16),
                        pltpu.VMEM((n_heads, nbp, hd), F32),
                        pltpu.VMEM((n_heads, hd + slots, tq), BF16)],
        compiler_params=_cparams(("arbitrary", "arbitrary")),
        name="prompt_attention",
    )(q_bf, k_bf, vtb)


def _head_diag(o, n_heads, hd, rows_per_head):
    row_head = lax.broadcasted_iota(jnp.int32, (o.shape[0], hd), 0) // rows_per_head
    out = jnp.zeros((o.shape[0], hd), F32)
    for h in range(n_heads):
        out = out + jnp.where(row_head == h, o[:, h * hd:(h + 1) * hd], 0.0)
    return out


def _sample_attn_kernel(pt_ref, q_ref, kn_ref, vn_ref, *rest, n_heads, hd, scale):
    del pt_ref
    npg = PAGES_PER_STEP
    k_pages = rest[:npg]
    v_pages = rest[npg:2 * npg]
    o_ref = rest[2 * npg]
    m_ref, l_ref, ksum_ref, acc_ref = rest[2 * npg + 1:]
    step = pl.program_id(1)
    n_steps = pl.num_programs(1)
    t_new = q_ref.shape[1]
    aw = n_heads * hd
    rows = n_heads * t_new
    page = k_pages[0].shape[4]
    pages_per_blk = MOBA_BLOCK // page
    n_past = acc_ref.shape[0]

    @pl.when(step == 0)
    def _init():
        m_ref[...] = jnp.zeros(m_ref.shape, F32)
        l_ref[...] = jnp.zeros(l_ref.shape, F32)
        ksum_ref[...] = jnp.zeros(ksum_ref.shape, F32)

    q_rep = jnp.concatenate([q_ref[0]] * n_heads, axis=0)
    own_head = (lax.broadcasted_iota(jnp.int32, (rows, aw), 0) // t_new
                == lax.broadcasted_iota(jnp.int32, (rows, aw), 1) // hd)
    q_bd = jnp.where(own_head, q_rep, 0.0)
    q_bd_bf = (q_bd * scale).astype(BF16)
    blk_lane = lax.broadcasted_iota(jnp.int32, (1, LANES), 1)
    ones_bf = jnp.ones((MOBA_BLOCK, LANES), BF16)

    n_blk = npg // pages_per_blk
    scored = []
    for bi in range(n_blk):
        pr = range(bi * pages_per_blk, (bi + 1) * pages_per_blk)
        kt_bf = jnp.concatenate([k_pages[r][0, 0].reshape(aw, page) for r in pr], axis=1).astype(BF16)
        scored.append((jnp.dot(q_bd_bf, kt_bf, preferred_element_type=F32),
                       jnp.dot(kt_bf, ones_bf, preferred_element_type=F32)))
    soft = []
    for s, _ in scored:
        m = jnp.max(s, axis=-1, keepdims=True)
        p = jnp.exp(s - m)
        soft.append((m, jnp.sum(p, axis=-1, keepdims=True), p.astype(BF16)))
    for bi in range(n_blk):
        pr = range(bi * pages_per_blk, (bi + 1) * pages_per_blk)
        vt_bf = jnp.concatenate([v_pages[r][0, 0].reshape(aw, page) for r in pr], axis=1).astype(BF16)
        m, l, p = soft[bi]
        o = _dot_nt(p, vt_bf)
        j = step * n_blk + bi
        here = blk_lane == j
        m_ref[...] = jnp.where(here, m, m_ref[...])
        l_ref[...] = jnp.where(here, l, l_ref[...])
        ksum_ref[...] = jnp.where(here, scored[bi][1], ksum_ref[...])
        acc_ref[j] = _head_diag(o, n_heads, hd, t_new)

    @pl.when(step == n_steps - 1)
    def _finish():
        s = _dot_nt(q_bd * scale, kn_ref[0])
        causal = (lax.broadcasted_iota(jnp.int32, (rows, t_new), 1)
                  <= lax.broadcasted_iota(jnp.int32, (rows, t_new), 0) % t_new)
        s = jnp.where(causal, s, -jnp.inf)
        m_own = jnp.max(s, axis=-1, keepdims=True)
        p = jnp.exp(s - m_own)
        l_own = jnp.sum(p, axis=-1, keepdims=True)
        o_own = _head_diag(jnp.dot(p, vn_ref[0], preferred_element_type=F32), n_heads, hd, t_new)

        gate = jnp.dot(q_bd, ksum_ref[...] * (1.0 / MOBA_BLOCK), preferred_element_type=F32, precision=HIGHEST)
        sel = _moba_select(gate, n_past, n_past, axis=1)
        m_all = m_ref[...]
        m_sel = jnp.maximum(m_own, jnp.max(jnp.where(sel, m_all, -jnp.inf), axis=-1, keepdims=True))
        w = jnp.where(sel, jnp.exp(m_all - m_sel), 0.0)
        w_own = jnp.exp(m_own - m_sel)
        den = l_own * w_own + jnp.sum(w * l_ref[...], axis=-1, keepdims=True)
        num = o_own * w_own
        for j in range(n_past):
            num = num + w[:, j:j + 1] * acc_ref[j]
        out = num / den
        o_ref[0] = jnp.concatenate([out[h * t_new:(h + 1) * t_new] for h in range(n_heads)], axis=1)


def _sample_attention(q, k_new, v_new, cache_kt, cache_vt, page_table, layer):
    n, t_new, aw = q.shape
    _, _, n_heads, hd, page = cache_kt.shape
    n_pages = page_table.shape[1]
    npg = PAGES_PER_STEP
    assert n_pages % npg == 0 and MOBA_BLOCK % page == 0 and npg % (MOBA_BLOCK // page) == 0
    n_past = n_pages * page // MOBA_BLOCK
    assert n_past <= LANES
    rows = n_heads * t_new

    def page_spec(r):
        return pl.BlockSpec((1, 1, n_heads, hd, page),
                            lambda ni, s, pt: (layer, pt[ni, s * npg + r], 0, 0, 0))

    new_spec = pl.BlockSpec((1, t_new, aw), lambda ni, s, pt: (ni, 0, 0))
    grid_spec = pltpu.PrefetchScalarGridSpec(
        num_scalar_prefetch=1,
        grid=(n, n_pages // npg),
        in_specs=[new_spec, new_spec, new_spec] + [page_spec(r) for r in range(npg)] * 2,
        out_specs=new_spec,
        scratch_shapes=[pltpu.VMEM((rows, LANES), F32),
                        pltpu.VMEM((rows, LANES), F32),
                        pltpu.VMEM((aw, LANES), F32),
                        pltpu.VMEM((n_past, rows, hd), F32)],
    )
    return pl.pallas_call(
        functools.partial(_sample_attn_kernel, n_heads=n_heads, hd=hd, scale=hd ** -0.5),
        grid_spec=grid_spec,
        out_shape=jax.ShapeDtypeStruct(q.shape, F32),
        compiler_params=_cparams(("arbitrary", "arbitrary")),
        name="sample_attention",
    )(page_table, q, k_new, v_new, *([cache_kt] * npg), *([cache_vt] * npg))


def _split_bf16(x):
    hi = x.astype(BF16)
    return hi, (x - hi.astype(F32)).astype(BF16)


def _post_kernel(att_ref, pz_ref, pre_ref, gab_ref, x_ref, gt1_ref, sh2_ref, sc2_ref,
                 wao_ref, wpool_ref, pscale_ref, wout_ref, gpost1_ref, gpre2_ref, wrh_ref, wrl_ref, br_ref,
                 x1_ref, h2_ref, topi_ref, gates_ref, *, pos0, zero_first_prefix):
    bb, tt, d = x_ref.shape
    rows = bb * tt
    i = pl.program_id(1)
    n_groups = len(POOL_WINDOWS)
    gc = pz_ref.shape[2] // n_groups

    pz = pz_ref[...]
    prefix = pre_ref[...]
    if zero_first_prefix:
        prefix = jnp.where(i == 0, 0.0, prefix)
    xp = jnp.concatenate([prefix, pz], axis=1)
    pos = pos0 + i * tt + lax.broadcasted_iota(jnp.int32, (1, tt, 1), 1)
    pbs = []
    for g, w in enumerate(POOL_WINDOWS):
        a = xp[:, :, g * gc:(g + 1) * gc]
        span = 1
        while span < w:
            a = a + pltpu.roll(a, span, 1)
            span *= 2
        win = a[:, POOL_PREFIX_ROWS:, :]
        cnt = jnp.minimum(w, pos + 1).astype(F32)
        pooled = win / cnt - pz[:, :, g * gc:(g + 1) * gc]
        pbs.append(jnp.dot(pooled.reshape(rows, gc).astype(BF16), wpool_ref[g], preferred_element_type=F32))
    pb = jnp.concatenate(pbs, axis=1) * pscale_ref[...]

    att = jnp.dot(att_ref[...].reshape(rows, att_ref.shape[2]).astype(BF16), wao_ref[...],
                  preferred_element_type=F32)
    gab = gab_ref[...].reshape(rows, 2 * d)
    mix = jax.nn.sigmoid(gab[:, :d]) * att + jax.nn.sigmoid(gab[:, d:]) * pb
    mix = jnp.dot(mix.astype(BF16), wout_ref[...], preferred_element_type=F32)
    x1 = x_ref[...] + gt1_ref[...] * _rms(mix, gpost1_ref[...]).reshape(bb, tt, d)
    x1_ref[...] = x1
    h2 = (_rms(x1, gpre2_ref[...]) * (1.0 + sc2_ref[...]) + sh2_ref[...]).reshape(rows, d)
    h2_ref[...] = h2.reshape(bb, tt, d)

    h_hi, h_lo = _split_bf16(h2)
    logits = (jnp.dot(h_hi, wrh_ref[...], preferred_element_type=F32)
              + (jnp.dot(h_lo, wrh_ref[...], preferred_element_type=F32)
                 + jnp.dot(h_hi, wrl_ref[...], preferred_element_type=F32))) + br_ref[...]
    n_exp = logits.shape[1]
    lane = lax.broadcasted_iota(jnp.int32, logits.shape, 1).astype(F32)
    kcol = lax.broadcasted_iota(jnp.int32, (rows, MOE_TOP_K), 1)
    vals = jnp.zeros((rows, MOE_TOP_K), F32)
    idxs = jnp.zeros((rows, MOE_TOP_K), F32)
    v = logits
    for k in range(MOE_TOP_K):
        mx = jnp.max(v, axis=-1, keepdims=True)
        ix = jnp.min(jnp.where(v == mx, lane, float(n_exp)), axis=-1, keepdims=True)
        vals = jnp.where(kcol == k, mx, vals)
        idxs = jnp.where(kcol == k, ix, idxs)
        v = jnp.where(lane == ix, -jnp.inf, v)
    e = jnp.exp(vals - vals[:, 0:1])
    gates = e / jnp.sum(e, axis=-1, keepdims=True)
    topi_ref[...] = idxs.astype(jnp.int32).reshape(bb, tt, MOE_TOP_K)
    gates_ref[...] = gates.reshape(bb, tt, MOE_TOP_K)


def _post(att, pz, prefix, gab, x, mod3, weights, bb, tt, pos0, zero_first_prefix):
    bx, t, d = x.shape
    aw, pw = att.shape[2], pz.shape[2]
    grid = (bx // bb, t // tt)
    row = lambda width: pl.BlockSpec((bb, tt, width), lambda b, i: (b, i, 0))
    modspec = lambda slot: pl.BlockSpec((bb, 1, d), lambda b, i: (b, 0, slot))
    full = lambda a: pl.BlockSpec(a.shape, lambda b, i: (0,) * a.ndim)
    if zero_first_prefix:
        per_tile = tt // POOL_PREFIX_ROWS
        pre_spec = pl.BlockSpec((bb, POOL_PREFIX_ROWS, pw), lambda b, i: (b, jnp.maximum(i * per_tile - 1, 0), 0))
    else:
        assert t == tt
        pre_spec = pl.BlockSpec((bb, POOL_PREFIX_ROWS, pw), lambda b, i: (b, 0, 0))
    out = lambda width, dt: jax.ShapeDtypeStruct((bx, t, width), dt)
    return pl.pallas_call(
        functools.partial(_post_kernel, pos0=pos0, zero_first_prefix=zero_first_prefix),
        grid=grid,
        in_specs=[row(aw), row(pw), pre_spec, row(2 * d), row(d), modspec(2), modspec(3), modspec(4)]
                 + [full(w) for w in weights],
        out_specs=[row(d), row(d), row(MOE_TOP_K), row(MOE_TOP_K)],
        out_shape=[out(d, F32), out(d, F32), out(MOE_TOP_K, jnp.int32), out(MOE_TOP_K, F32)],
        compiler_params=_cparams(("arbitrary", "arbitrary")),
        name="post",
    )(att, pz, prefix, gab, x, mod3, mod3, mod3, *weights)


def _plan_kernel(topi_ref, rank_ref, counts_ref, base_ref):
    i = pl.program_id(0)
    tt = topi_ref.shape[0]
    n_exp = counts_ref.shape[1]

    @pl.when(i == 0)
    def _init():
        base_ref[...] = jnp.zeros(base_ref.shape, F32)

    topi = topi_ref[...]
    lane = lax.broadcasted_iota(jnp.int32, (tt, n_exp), 1)
    onehots = [jnp.where(topi[:, k:k + 1] == lane, 1.0, 0.0) for k in range(MOE_TOP_K)]
    tot = onehots[0]
    for oh in onehots[1:]:
        tot = tot + oh
    tri = jnp.where(lax.broadcasted_iota(jnp.int32, (tt, tt), 0) > lax.broadcasted_iota(jnp.int32, (tt, tt), 1),
                    1.0, 0.0).astype(BF16)
    before = jnp.dot(tri, tot.astype(BF16), preferred_element_type=F32) + base_ref[...]
    kcol = lax.broadcasted_iota(jnp.int32, (tt, MOE_TOP_K), 1)
    rank = jnp.zeros((tt, MOE_TOP_K), F32)
    for k in range(MOE_TOP_K):
        rank = jnp.where(kcol == k, jnp.sum(onehots[k] * before, axis=-1, keepdims=True), rank)
    rank_ref[...] = rank.astype(jnp.int32)
    base_ref[...] = base_ref[...] + jnp.sum(tot, axis=0, keepdims=True)
    counts_ref[...] = base_ref[...].astype(jnp.int32)


def _moe_plan(topi, n_exp):
    t = topi.shape[0]
    tt = PLAN_ROWS
    assert t % tt == 0
    return pl.pallas_call(
        _plan_kernel,
        grid=(t // tt,),
        in_specs=[pl.BlockSpec((tt, MOE_TOP_K), lambda i: (i, 0))],
        out_specs=[pl.BlockSpec((tt, MOE_TOP_K), lambda i: (i, 0)),
                   pl.BlockSpec((1, n_exp), lambda i: (0, 0))],
        out_shape=[jax.ShapeDtypeStruct((t, MOE_TOP_K), jnp.int32),
                   jax.ShapeDtypeStruct((1, n_exp), jnp.int32)],
        scratch_shapes=[pltpu.VMEM((1, n_exp), F32)],
        compiler_params=_cparams(("arbitrary",)),
        name="moe_plan",
    )(topi)


def _dispatch_kernel(dest_ref, h_ref, xs_in_ref, xs_ref, sem):
    del xs_in_ref
    tt = h_ref.shape[0]

    def row_copy(r, k):
        return pltpu.make_async_copy(h_ref.at[pl.ds(r, 1), :],
                                     xs_ref.at[pl.ds(dest_ref[0, 0, r * MOE_TOP_K + k], 1), :], sem)

    def issue(r, carry):
        for k in range(MOE_TOP_K):
            row_copy(r, k).start()
        return carry

    lax.fori_loop(0, tt, issue, 0)
    for k in range(MOE_TOP_K):
        pltpu.make_async_copy(h_ref, xs_ref.at[pl.ds(0, tt), :], sem).wait()


def _moe_dispatch(dest3, h2_all, xs_init):
    t, d = h2_all.shape
    tt = DISPATCH_ROWS
    assert t % tt == 0
    return pl.pallas_call(
        _dispatch_kernel,
        grid=(t // tt,),
        in_specs=[pl.BlockSpec((1, 1, tt * MOE_TOP_K), lambda i: (i, 0, 0), memory_space=pltpu.SMEM),
                  pl.BlockSpec((tt, d), lambda i: (i, 0)),
                  pl.BlockSpec(memory_space=pl.ANY)],
        out_specs=pl.BlockSpec(memory_space=pl.ANY),
        out_shape=jax.ShapeDtypeStruct(xs_init.shape, F32),
        scratch_shapes=[pltpu.SemaphoreType.DMA(())],
        input_output_aliases={2: 0},
        compiler_params=_cparams(("arbitrary",)),
        name="moe_dispatch",
    )(dest3, h2_all, xs_init)


def _expert_kernel(blk_e_ref, n_used_ref, xs_ref, wgu_ref, bgu_ref, wd_ref, bd_ref, ys_ref):
    del blk_e_ref
    i = pl.program_id(0)
    dff = wd_ref.shape[1]

    @pl.when(i < n_used_ref[0])
    def _compute():
        x = xs_ref[...].astype(BF16)
        gu = jnp.dot(x, wgu_ref[0], preferred_element_type=F32) + bgu_ref[0]
        gt = jnp.minimum(gu[:, :dff], SWIGLU_LIMIT)
        up = jnp.clip(gu[:, dff:], -SWIGLU_LIMIT, SWIGLU_LIMIT)
        act = (up + 1.0) * (gt * jax.nn.sigmoid(gt * SWIGLU_ALPHA))
        ys_ref[...] = jnp.dot(act.astype(BF16), wd_ref[0], preferred_element_type=F32) + bd_ref[0]

    @pl.when(i >= n_used_ref[0])
    def _skip():
        ys_ref[...] = jnp.zeros(ys_ref.shape, F32)


def _moe_experts(blk_e, n_used, xs, w_gu_bf, b_gu3, w_down_bf, b_down3):
    p, d = xs.shape
    bm = EXPERT_ROWS
    n_exp, _, two_f = w_gu_bf.shape
    dff = w_down_bf.shape[1]
    grid_spec = pltpu.PrefetchScalarGridSpec(
        num_scalar_prefetch=2,
        grid=(p // bm,),
        in_specs=[pl.BlockSpec((bm, d), lambda i, be, nu: (i, 0)),
                  pl.BlockSpec((1, d, two_f), lambda i, be, nu: (be[i], 0, 0)),
                  pl.BlockSpec((1, 1, two_f), lambda i, be, nu: (be[i], 0, 0)),
                  pl.BlockSpec((1, dff, d), lambda i, be, nu: (be[i], 0, 0)),
                  pl.BlockSpec((1, 1, d), lambda i, be, nu: (be[i], 0, 0))],
        out_specs=pl.BlockSpec((bm, d), lambda i, be, nu: (i, 0)),
    )
    return pl.pallas_call(
        _expert_kernel,
        grid_spec=grid_spec,
        out_shape=jax.ShapeDtypeStruct((p, d), F32),
        compiler_params=_cparams(("arbitrary",)),
        name="moe_experts",
    )(blk_e, n_used, xs, w_gu_bf, b_gu3, w_down_bf, b_down3)


def _combine_kernel(dest_ref, gates_ref, x1_ref, gt2_ref, gpost2_ref, ys_ref, y_ref, buf_ref, sem):
    bb, tt, d = x1_ref.shape
    rows = bb * tt

    def row_copy(r, k):
        return pltpu.make_async_copy(ys_ref.at[pl.ds(dest_ref[0, 0, r * MOE_TOP_K + k], 1), :],
                                     buf_ref.at[k, pl.ds(r, 1), :], sem)

    def issue(r, carry):
        for k in range(MOE_TOP_K):
            row_copy(r, k).start()
        return carry

    lax.fori_loop(0, rows, issue, 0)
    for k in range(MOE_TOP_K):
        pltpu.make_async_copy(ys_ref.at[pl.ds(0, rows), :], buf_ref.at[k], sem).wait()

    gates = gates_ref[...].reshape(rows, MOE_TOP_K)
    ff = gates[:, 0:1] * buf_ref[0]
    for k in range(1, MOE_TOP_K):
        ff = ff + gates[:, k:k + 1] * buf_ref[k]
    y_ref[...] = x1_ref[...] + gt2_ref[...] * _rms(ff, gpost2_ref[...]).reshape(bb, tt, d)


def _moe_combine(dest3, gates, x1, mod3, g_post2, ys, bb, tt):
    bx, t, d = x1.shape
    rows = bb * tt
    n_t = t // tt
    row = lambda width: pl.BlockSpec((bb, tt, width), lambda b, i: (b, i, 0))
    return pl.pallas_call(
        _combine_kernel,
        grid=(bx // bb, n_t),
        in_specs=[pl.BlockSpec((1, 1, rows * MOE_TOP_K), lambda b, i: (b * n_t + i, 0, 0), memory_space=pltpu.SMEM),
                  row(MOE_TOP_K), row(d),
                  pl.BlockSpec((bb, 1, d), lambda b, i: (b, 0, 5)),
                  pl.BlockSpec((1, d), lambda b, i: (0, 0)),
                  pl.BlockSpec(memory_space=pl.ANY)],
        out_specs=row(d),
        out_shape=jax.ShapeDtypeStruct((bx, t, d), F32),
        scratch_shapes=[pltpu.VMEM((MOE_TOP_K, rows, d), F32), pltpu.SemaphoreType.DMA(())],
        compiler_params=_cparams(("arbitrary", "arbitrary")),
        name="moe_combine",
    )(dest3, gates, x1, mod3, g_post2, ys)


def _layer(xp, xs, cp, cs, cache_kt, cache_vt, state_pool_l, page_table, layer, p):
    b, s_len, d = xp.shape
    n, t_new, _ = xs.shape
    n_heads, hd, page = cache_kt.shape[2:]
    aw = n_heads * hd
    pw = state_pool_l.shape[2]
    n_exp = p['w_router'].shape[1]
    sb = min(SAMPLE_SEQS, n)
    pr = min(PROMPT_ROWS, s_len)

    mod = _mod(jnp.concatenate([cp, cs], axis=0), p['w_mod'], p['b_mod'])
    mod_p = mod[:b].reshape(b, 1, N_MOD * d)
    mod_s = mod[b:].reshape(n, 1, N_MOD * d)

    w_in_bf = p['w_in'].astype(BF16)
    w_kvt_bf = p['w_in'][:, aw:3 * aw].T.astype(BF16)
    qp, kp, ktp, vtp, vtbp, pzp, gabp = _in_proj_prompt(xp, mod_p, p['g_pre1'], w_in_bf, w_kvt_bf, aw, pw, pr)
    qs, ks, vs, pzs, gabs = _in_proj_sample(xs, mod_s, p['g_pre1'], w_in_bf, aw, pw, sb)

    att_p = _prompt_attention(qp, kp, vtbp, n_heads, hd)
    att_s = _sample_attention(qs, ks, vs, cache_kt, cache_vt, page_table, layer)

    wr_hi = p['w_router'].astype(BF16)
    wr_lo = (p['w_router'] - wr_hi.astype(F32)).astype(BF16)
    post_w = (p['w_attn_o'].astype(BF16), p['w_pool'].astype(BF16), p['pool_scale'], p['w_out'].astype(BF16),
              p['g_post1'], p['g_pre2'], wr_hi, wr_lo, p['b_router'])
    x1p, h2p, tip, gp = _post(att_p, pzp, pzp, gabp, xp, mod_p, post_w, 1, pr, 0, True)
    state16 = jnp.concatenate([jnp.zeros((n, 1, pw), F32), state_pool_l], axis=1)
    x1s, h2s, tis, gs = _post(att_s, pzs, state16, gabs, xs, mod_s, post_w, sb, t_new,
                              page_table.shape[1] * page, False)

    tp, ts = b * s_len, n * t_new
    h2_all = jnp.concatenate([h2p.reshape(tp, d), h2s.reshape(ts, d)], axis=0)
    topi = jnp.concatenate([tip.reshape(tp, MOE_TOP_K), tis.reshape(ts, MOE_TOP_K)], axis=0)
    t_all = tp + ts
    rank, counts = _moe_plan(topi, n_exp)
    bm = EXPERT_ROWS
    counts = counts[0]
    padded = (counts + bm - 1) // bm * bm
    pends = jnp.cumsum(padded)
    pstarts = pends - padded
    n_blocks = (t_all * MOE_TOP_K + n_exp * (bm - 1) + bm - 1) // bm
    blk_start = jnp.arange(n_blocks, dtype=jnp.int32) * bm
    blk_e = jnp.minimum(jnp.sum((pends[None, :] <= blk_start[:, None]).astype(jnp.int32), axis=1), n_exp - 1)
    n_used = (pends[-1:] // bm).astype(jnp.int32)
    onehot_e = topi[:, :, None] == jnp.arange(n_exp, dtype=jnp.int32)
    dest = jnp.sum(jnp.where(onehot_e, pstarts.astype(jnp.int32), 0), axis=-1) + rank

    xs_sorted = _moe_dispatch(dest.reshape(t_all // DISPATCH_ROWS, 1, DISPATCH_ROWS * MOE_TOP_K), h2_all,
                              jnp.zeros((n_blocks * bm, d), F32))
    ys = _moe_experts(blk_e, n_used, xs_sorted, p['w_gu'].astype(BF16), p['b_gu'][:, None, :],
                      p['w_down'].astype(BF16), p['b_down'][:, None, :])

    cr = min(COMBINE_ROWS, s_len)
    csb = min(max(COMBINE_ROWS // t_new, 1), n)
    dest_p = dest[:tp].reshape(tp // cr, 1, cr * MOE_TOP_K)
    dest_s = dest[tp:].reshape(ts // (csb * t_new), 1, csb * t_new * MOE_TOP_K)
    yp = _moe_combine(dest_p, gp, x1p, mod_p, p['g_post2'], ys, 1, cr)
    ysm = _moe_combine(dest_s, gs, x1s, mod_s, p['g_post2'], ys, csb, t_new)

    k_out = ktp.reshape(b, n_heads, hd, s_len).transpose(0, 3, 1, 2)
    v_out = vtp.reshape(b, n_heads, hd, s_len).transpose(0, 3, 1, 2)
    pool_p = pzp[:, s_len - (POOL_PREFIX_ROWS - 1):, :]
    pool_s = jnp.concatenate([state_pool_l, pzs], axis=1)[:, -(POOL_PREFIX_ROWS - 1):, :]
    return (yp, ysm, k_out, v_out, pool_p,
            ks.reshape(n, t_new, n_heads, hd), vs.reshape(n, t_new, n_heads, hd), pool_s)


def kernel(x_prompt, x_sample, c_prompt, c_sample, cache_k, cache_v, state_pool, page_table, w_mod, b_mod, g_pre1, g_post1, g_pre2, g_post2, w_in, w_attn_o, w_pool, pool_scale, w_out, w_router, b_router, w_gu, b_gu, w_down, b_down):
    depth = w_mod.shape[0]
    hp, hs = x_prompt, x_sample
    cache_kt = cache_k.transpose(0, 1, 3, 4, 2)
    cache_vt = cache_v.transpose(0, 1, 3, 4, 2)
    outs = [[] for _ in range(6)]
    for l in range(depth):
        p = {'w_mod': w_mod[l], 'b_mod': b_mod[l][None, :], 'g_pre1': g_pre1[l][None, :],
             'g_post1': g_post1[l][None, :], 'g_pre2': g_pre2[l][None, :], 'g_post2': g_post2[l][None, :],
             'w_in': w_in[l], 'w_attn_o': w_attn_o[l], 'w_pool': w_pool[l], 'pool_scale': pool_scale[l][None, :],
             'w_out': w_out[l], 'w_router': w_router[l], 'b_router': b_router[l][None, :],
             'w_gu': w_gu[l], 'b_gu': b_gu[l], 'w_down': w_down[l], 'b_down': b_down[l]}
        hp, hs, kp, vp, pp, ks, vs, ps = _layer(hp, hs, c_prompt, c_sample, cache_kt, cache_vt, state_pool[l],
                                                page_table, l, p)
        for lst, val in zip(outs, (kp, vp, pp, ks, vs, ps)):
            lst.append(val)
    return (hp, hs) + tuple(jnp.stack(o) for o in outs)
```

```python
import functools

import jax
import jax.numpy as jnp
from jax import lax
from jax.experimental import pallas as pl
from jax.experimental.pallas import tpu as pltpu

F32 = jnp.float32
BF16 = jnp.bfloat16
HIGHEST = lax.Precision.HIGHEST

MOBA_BLOCK = 256
MOBA_TOPK = 3
POOL_WINDOWS = (2, 4, 8, 16)
POOL_PREFIX_ROWS = 16
MOE_TOP_K = 4
SWIGLU_LIMIT = 7.0
SWIGLU_ALPHA = 1.702
N_MOD = 6
NORM_EPS = 1e-6
MASK_BIAS = -1e30

LANES = 128
SUBLANES = 8
VMEM_LIMIT = 56 * 1024 * 1024
PROMPT_ROWS = 512
SAMPLE_SEQS = 16
PAGES_PER_STEP = 8
PLAN_ROWS = 512
DISPATCH_ROWS = 512
EXPERT_ROWS = 512
COMBINE_ROWS = 256
ISSUE_UNROLL = 4


def _cparams(sem):
    return pltpu.CompilerParams(dimension_semantics=sem, vmem_limit_bytes=VMEM_LIMIT)


def _rms(x, g):
    return x * lax.rsqrt(jnp.mean(x * x, axis=-1, keepdims=True) + NORM_EPS) * g


def _dot_nt(a, b, precision=None):
    return lax.dot_general(a, b, (((1,), (1,)), ((), ())), preferred_element_type=F32, precision=precision)


def _sigmoid(x):
    return 0.5 * jnp.tanh(0.5 * x) + 0.5


def _load_tile_rows(ref, rows, lead=()):
    return jnp.concatenate([ref[lead + (pl.ds(c, rows, stride=SUBLANES), slice(None))] for c in range(SUBLANES)],
                           axis=1)


def _store_tile_rows(ref, x):
    rows = x.shape[0]
    for c in range(SUBLANES):
        ref[pl.ds(c, rows, stride=SUBLANES), :] = x[:, c * LANES:(c + 1) * LANES]


def _mod_kernel(c_ref, w_ref, b_ref, o_ref):
    c = c_ref[...]
    s = c * jax.nn.sigmoid(c)
    o_ref[...] = jnp.dot(s, w_ref[...], preferred_element_type=F32, precision=HIGHEST) + b_ref[...]


def _mod(c_all, w_mod, b_mod):
    n, d = c_all.shape
    return pl.pallas_call(
        _mod_kernel,
        grid=(N_MOD,),
        in_specs=[pl.BlockSpec((n, d), lambda j: (0, 0)),
                  pl.BlockSpec((d, d), lambda j: (0, j)),
                  pl.BlockSpec((1, d), lambda j: (0, j))],
        out_specs=pl.BlockSpec((n, d), lambda j: (0, j)),
        out_shape=jax.ShapeDtypeStruct((n, N_MOD * d), F32),
        compiler_params=_cparams(("arbitrary",)),
        name="mod",
    )(c_all, w_mod, b_mod)


def _modulated(x_ref, sh_ref, sc_ref, g_ref):
    bb, tt, d = x_ref.shape
    h = _rms(x_ref[...], g_ref[...]) * (1.0 + sc_ref[...]) + sh_ref[...]
    return h.reshape(bb * tt, d).astype(BF16)


def _in_proj_sample_kernel(x_ref, sh_ref, sc_ref, g_ref, w_ref, q_ref, k_ref, v_ref, pz_ref, gab_ref, *, aw, pw):
    bb, tt, d = x_ref.shape
    hb = _modulated(x_ref, sh_ref, sc_ref, g_ref)
    col = 0
    for ref, width in ((q_ref, aw), (k_ref, aw), (v_ref, aw), (pz_ref, pw), (gab_ref, 2 * d)):
        u = jnp.dot(hb, w_ref[:, col:col + width], preferred_element_type=F32)
        ref[...] = u.reshape(bb, tt, width)
        col += width


def _in_proj_prompt_kernel(x_ref, sh_ref, sc_ref, g_ref, w_ref, wkvt_ref,
                           q_ref, k_ref, kt_ref, vt_ref, vtb_ref, pz_ref, gab_ref, *, aw, pw):
    _, tt, d = x_ref.shape
    hb = _modulated(x_ref, sh_ref, sc_ref, g_ref)
    q_ref[0] = jnp.dot(hb, w_ref[:, 0:aw], preferred_element_type=F32).astype(BF16)
    k_ref[0] = jnp.dot(hb, w_ref[:, aw:2 * aw], preferred_element_type=F32).astype(BF16)
    col = 3 * aw
    pz_ref[0] = jnp.dot(hb, w_ref[:, col:col + pw], preferred_element_type=F32)
    gab_ref[0] = jnp.dot(hb, w_ref[:, col + pw:col + pw + 2 * d], preferred_element_type=F32)
    kvt = _dot_nt(wkvt_ref[...], hb)
    kt_ref[0] = kvt[0:aw]
    vt_ref[0] = kvt[aw:2 * aw]
    for j in range(tt // MOBA_BLOCK):
        vtb_ref[0, j] = kvt[aw:2 * aw, j * MOBA_BLOCK:(j + 1) * MOBA_BLOCK].astype(BF16)


def _in_proj_sample(x, mod3, g_pre1, w_in_bf, aw, pw, bb):
    bx, t, d = x.shape
    row = lambda width: pl.BlockSpec((bb, t, width), lambda b: (b, 0, 0))
    modspec = lambda slot: pl.BlockSpec((bb, 1, d), lambda b: (b, 0, slot))
    out = lambda width: jax.ShapeDtypeStruct((bx, t, width), F32)
    return pl.pallas_call(
        functools.partial(_in_proj_sample_kernel, aw=aw, pw=pw),
        grid=(bx // bb,),
        in_specs=[row(d), modspec(0), modspec(1),
                  pl.BlockSpec((1, d), lambda b: (0, 0)),
                  pl.BlockSpec(w_in_bf.shape, lambda b: (0, 0))],
        out_specs=[row(aw), row(aw), row(aw), row(pw), row(2 * d)],
        out_shape=[out(aw), out(aw), out(aw), out(pw), out(2 * d)],
        compiler_params=_cparams(("arbitrary",)),
        name="in_proj_sample",
    )(x, mod3, mod3, g_pre1, w_in_bf)


def _in_proj_prompt(x, mod3, g_pre1, w_in_bf, w_kvt_bf, aw, pw, tt):
    bx, t, d = x.shape
    assert tt % MOBA_BLOCK == 0 and t % tt == 0
    nb, per = t // MOBA_BLOCK, tt // MOBA_BLOCK
    row = lambda width: pl.BlockSpec((1, tt, width), lambda b, i: (b, i, 0))
    col = pl.BlockSpec((1, aw, tt), lambda b, i: (b, 0, i))
    modspec = lambda slot: pl.BlockSpec((1, 1, d), lambda b, i: (b, 0, slot))
    return pl.pallas_call(
        functools.partial(_in_proj_prompt_kernel, aw=aw, pw=pw),
        grid=(bx, t // tt),
        in_specs=[row(d), modspec(0), modspec(1),
                  pl.BlockSpec((1, d), lambda b, i: (0, 0)),
                  pl.BlockSpec(w_in_bf.shape, lambda b, i: (0, 0)),
                  pl.BlockSpec(w_kvt_bf.shape, lambda b, i: (0, 0))],
        out_specs=[row(aw), row(aw), col, col,
                   pl.BlockSpec((1, per, aw, MOBA_BLOCK), lambda b, i: (b, i, 0, 0)),
                   row(pw), row(2 * d)],
        out_shape=[jax.ShapeDtypeStruct((bx, t, aw), BF16), jax.ShapeDtypeStruct((bx, t, aw), BF16),
                   jax.ShapeDtypeStruct((bx, aw, t), F32), jax.ShapeDtypeStruct((bx, aw, t), F32),
                   jax.ShapeDtypeStruct((bx, nb, aw, MOBA_BLOCK), BF16),
                   jax.ShapeDtypeStruct((bx, t, pw), F32), jax.ShapeDtypeStruct((bx, t, 2 * d), F32)],
        compiler_params=_cparams(("arbitrary", "arbitrary")),
        name="in_proj_prompt",
    )(x, mod3, mod3, g_pre1, w_in_bf, w_kvt_bf)


def _moba_select(gate, n_past, n_cand, axis):
    slot = lax.broadcasted_iota(jnp.int32, gate.shape, axis)
    past = slot < n_past
    g = jnp.where(past, gate, -jnp.inf)
    rank = jnp.zeros(gate.shape, jnp.int32)
    for j in range(n_cand):
        other = g[:, j:j + 1] if axis == 1 else g[j:j + 1, :]
        beats = (other > g) | ((other == g) & (j < slot))
        rank = rank + jnp.where(beats & (j < n_past), 1, 0)
    return past & (rank < MOBA_TOPK)


def _prompt_attn_kernel(q_ref, k_ref, vtb_ref, o_ref, kaug_ref, kmean_ref, qaug_ref, *, n_heads, hd, scale):
    i = pl.program_id(1)
    s_len = k_ref.shape[1]
    tq = q_ref.shape[1]
    nb = s_len // MOBA_BLOCK
    nbp = kmean_ref.shape[1]
    slots = kaug_ref.shape[2] - hd

    @pl.when(i == 0)
    def _prepare():
        kf = k_ref[0].astype(F32)
        key_blk = lax.broadcasted_iota(jnp.int32, (s_len, slots), 0) // MOBA_BLOCK
        onehot = jnp.where(key_blk == lax.broadcasted_iota(jnp.int32, (s_len, slots), 1), 1.0, 0.0).astype(BF16)
        kmean = kf.reshape(nb, MOBA_BLOCK, kf.shape[1]).sum(axis=1) * (1.0 / MOBA_BLOCK)
        kmean_ref[...] = jnp.zeros(kmean_ref.shape, F32)
        for h in range(n_heads):
            kaug_ref[h, :, 0:hd] = k_ref[0, :, h * hd:(h + 1) * hd]
            kaug_ref[h, :, hd:hd + slots] = onehot
            kmean_ref[h, 0:nb, :] = kmean[:, h * hd:(h + 1) * hd]

    qt = q_ref[0].astype(F32).T
    slot = lax.broadcasted_iota(jnp.int32, (nbp, tq), 0)
    gates = [jnp.dot(kmean_ref[h], qt[h * hd:(h + 1) * hd], preferred_element_type=F32, precision=HIGHEST)
             for h in range(n_heads)]
    for h in range(n_heads):
        sel = _moba_select(gates[h], i, nb, axis=0) | (slot == i)
        bias = jnp.where(sel, 0.0, MASK_BIAS)
        parts = [qt[h * hd:(h + 1) * hd] * scale, bias]
        if slots > nbp:
            parts.append(jnp.zeros((slots - nbp, tq), F32))
        qaug_ref[h] = jnp.concatenate(parts, axis=0).astype(BF16)

    def scores(h, r0):
        return jnp.dot(kaug_ref[h, pl.ds(r0, MOBA_BLOCK), :], qaug_ref[h], preferred_element_type=F32)

    key_id = lax.broadcasted_iota(jnp.int32, (MOBA_BLOCK, tq), 0)
    qry_id = lax.broadcasted_iota(jnp.int32, (MOBA_BLOCK, tq), 1)
    diag0 = pl.multiple_of(i * MOBA_BLOCK, MOBA_BLOCK)
    ss = [scores(h, diag0) for h in range(n_heads)]
    soft = []
    for h in range(n_heads):
        s = jnp.where(key_id <= qry_id, ss[h], -jnp.inf)
        m = jnp.max(s, axis=0, keepdims=True)
        p = jnp.exp(s - m)
        soft.append((m, jnp.sum(p, axis=0, keepdims=True), p.astype(BF16)))
    state = [(m, l, jnp.dot(vtb_ref[0, i, h * hd:(h + 1) * hd, :], p, preferred_element_type=F32))
             for h, (m, l, p) in enumerate(soft)]

    def body(j, state):
        r0 = pl.multiple_of(j * MOBA_BLOCK, MOBA_BLOCK)
        ss = [scores(h, r0) for h in range(n_heads)]
        soft = []
        for h in range(n_heads):
            m_old, l_old, _ = state[h]
            m_new = jnp.maximum(m_old, jnp.max(ss[h], axis=0, keepdims=True))
            alpha = jnp.exp(m_old - m_new)
            p = jnp.exp(ss[h] - m_new)
            soft.append((m_new, alpha, alpha * l_old + jnp.sum(p, axis=0, keepdims=True), p.astype(BF16)))
        new_state = []
        for h in range(n_heads):
            m_new, alpha, l_new, p = soft[h]
            acc = alpha * state[h][2] + jnp.dot(vtb_ref[0, j, h * hd:(h + 1) * hd, :], p,
                                                preferred_element_type=F32)
            new_state.append((m_new, l_new, acc))
        return tuple(new_state)

    state = lax.fori_loop(0, i, body, tuple(state))
    o_ref[0] = jnp.concatenate([acc / l for _, l, acc in state], axis=0).T


def _prompt_attention(q_bf, k_bf, vtb, n_heads, hd):
    b, s_len, aw = q_bf.shape
    assert s_len % MOBA_BLOCK == 0
    nb = s_len // MOBA_BLOCK
    nbp = -(-nb // SUBLANES) * SUBLANES
    slots = LANES - hd
    assert nbp <= slots
    tq = MOBA_BLOCK
    return pl.pallas_call(
        functools.partial(_prompt_attn_kernel, n_heads=n_heads, hd=hd, scale=hd ** -0.5),
        grid=(b, s_len // tq),
        in_specs=[pl.BlockSpec((1, tq, aw), lambda bi, i: (bi, i, 0)),
                  pl.BlockSpec((1, s_len, aw), lambda bi, i: (bi, 0, 0)),
                  pl.BlockSpec((1, nb, aw, MOBA_BLOCK), lambda bi, i: (bi, 0, 0, 0))],
        out_specs=pl.BlockSpec((1, tq, aw), lambda bi, i: (bi, i, 0)),
        out_shape=jax.ShapeDtypeStruct((b, s_len, aw), F32),
        scratch_shapes=[pltpu.VMEM((n_heads, s_len, hd + slots), BF16),
                        pltpu.VMEM((n_heads, nbp, hd), F32),
                        pltpu.VMEM((n_heads, hd + slots, tq), BF16)],
        compiler_params=_cparams(("arbitrary", "arbitrary")),
        name="prompt_attention",
    )(q_bf, k_bf, vtb)


def _head_diag(o, n_heads, hd, rows_per_head):
    row_head = lax.broadcasted_iota(jnp.int32, (o.shape[0], hd), 0) // rows_per_head
    out = jnp.zeros((o.shape[0], hd), F32)
    for h in range(n_heads):
        out = out + jnp.where(row_head == h, o[:, h * hd:(h + 1) * hd], 0.0)
    return out


def _sample_attn_kernel(pt_ref, q_ref, kn_ref, vn_ref, *rest, n_heads, hd, scale):
    del pt_ref
    npg = PAGES_PER_STEP
    k_pages = rest[:npg]
    v_pages = rest[npg:2 * npg]
    o_ref = rest[2 * npg]
    m_ref, l_ref, ksum_ref, acc_ref = rest[2 * npg + 1:]
    step = pl.program_id(1)
    n_steps = pl.num_programs(1)
    t_new = q_ref.shape[1]
    aw = n_heads * hd
    rows = n_heads * t_new
    page = k_pages[0].shape[4]
    pages_per_blk = MOBA_BLOCK // page
    n_past = acc_ref.shape[0]

    @pl.when(step == 0)
    def _init():
        m_ref[...] = jnp.zeros(m_ref.shape, F32)
        l_ref[...] = jnp.zeros(l_ref.shape, F32)
        ksum_ref[...] = jnp.zeros(ksum_ref.shape, F32)

    q_rep = jnp.concatenate([q_ref[0]] * n_heads, axis=0)
    own_head = (lax.broadcasted_iota(jnp.int32, (rows, aw), 0) // t_new
                == lax.broadcasted_iota(jnp.int32, (rows, aw), 1) // hd)
    q_bd = jnp.where(own_head, q_rep, 0.0)
    q_bd_bf = (q_bd * scale).astype(BF16)
    blk_lane = lax.broadcasted_iota(jnp.int32, (1, LANES), 1)
    ones_bf = jnp.ones((MOBA_BLOCK, LANES), BF16)

    n_blk = npg // pages_per_blk
    scored = []
    for bi in range(n_blk):
        pr = range(bi * pages_per_blk, (bi + 1) * pages_per_blk)
        kt_bf = jnp.concatenate([k_pages[r][0, 0].reshape(aw, page) for r in pr], axis=1).astype(BF16)
        scored.append((jnp.dot(q_bd_bf, kt_bf, preferred_element_type=F32),
                       jnp.dot(kt_bf, ones_bf, preferred_element_type=F32)))
    soft = []
    for s, _ in scored:
        m = jnp.max(s, axis=-1, keepdims=True)
        p = jnp.exp(s - m)
        soft.append((m, jnp.sum(p, axis=-1, keepdims=True), p.astype(BF16)))
    for bi in range(n_blk):
        pr = range(bi * pages_per_blk, (bi + 1) * pages_per_blk)
        vt_bf = jnp.concatenate([v_pages[r][0, 0].reshape(aw, page) for r in pr], axis=1).astype(BF16)
        m, l, p = soft[bi]
        o = _dot_nt(p, vt_bf)
        j = step * n_blk + bi
        here = blk_lane == j
        m_ref[...] = jnp.where(here, m, m_ref[...])
        l_ref[...] = jnp.where(here, l, l_ref[...])
        ksum_ref[...] = jnp.where(here, scored[bi][1], ksum_ref[...])
        acc_ref[j] = _head_diag(o, n_heads, hd, t_new)

    @pl.when(step == n_steps - 1)
    def _finish():
        s = _dot_nt(q_bd * scale, kn_ref[0])
        causal = (lax.broadcasted_iota(jnp.int32, (rows, t_new), 1)
                  <= lax.broadcasted_iota(jnp.int32, (rows, t_new), 0) % t_new)
        s = jnp.where(causal, s, -jnp.inf)
        m_own = jnp.max(s, axis=-1, keepdims=True)
        p = jnp.exp(s - m_own)
        l_own = jnp.sum(p, axis=-1, keepdims=True)
        o_own = _head_diag(jnp.dot(p, vn_ref[0], preferred_element_type=F32), n_heads, hd, t_new)

        gate = jnp.dot(q_bd, ksum_ref[...] * (1.0 / MOBA_BLOCK), preferred_element_type=F32, precision=HIGHEST)
        sel = _moba_select(gate, n_past, n_past, axis=1)
        m_all = m_ref[...]
        m_sel = jnp.maximum(m_own, jnp.max(jnp.where(sel, m_all, -jnp.inf), axis=-1, keepdims=True))
        w = jnp.where(sel, jnp.exp(m_all - m_sel), 0.0)
        w_own = jnp.exp(m_own - m_sel)
        den = l_own * w_own + jnp.sum(w * l_ref[...], axis=-1, keepdims=True)
        num = o_own * w_own
        for j in range(n_past):
            num = num + w[:, j:j + 1] * acc_ref[j]
        out = num / den
        o_ref[0] = jnp.concatenate([out[h * t_new:(h + 1) * t_new] for h in range(n_heads)], axis=1)


def _sample_attention(q, k_new, v_new, cache_kt, cache_vt, page_table, layer):
    n, t_new, aw = q.shape
    _, _, n_heads, hd, page = cache_kt.shape
    n_pages = page_table.shape[1]
    npg = PAGES_PER_STEP
    assert n_pages % npg == 0 and MOBA_BLOCK % page == 0 and npg % (MOBA_BLOCK // page) == 0
    n_past = n_pages * page // MOBA_BLOCK
    assert n_past <= LANES
    rows = n_heads * t_new

    def page_spec(r):
        return pl.BlockSpec((1, 1, n_heads, hd, page),
                            lambda ni, s, pt: (layer, pt[ni, s * npg + r], 0, 0, 0))

    new_spec = pl.BlockSpec((1, t_new, aw), lambda ni, s, pt: (ni, 0, 0))
    grid_spec = pltpu.PrefetchScalarGridSpec(
        num_scalar_prefetch=1,
        grid=(n, n_pages // npg),
        in_specs=[new_spec, new_spec, new_spec] + [page_spec(r) for r in range(npg)] * 2,
        out_specs=new_spec,
        scratch_shapes=[pltpu.VMEM((rows, LANES), F32),
                        pltpu.VMEM((rows, LANES), F32),
                        pltpu.VMEM((aw, LANES), F32),
                        pltpu.VMEM((n_past, rows, hd), F32)],
    )
    return pl.pallas_call(
        functools.partial(_sample_attn_kernel, n_heads=n_heads, hd=hd, scale=hd ** -0.5),
        grid_spec=grid_spec,
        out_shape=jax.ShapeDtypeStruct(q.shape, F32),
        compiler_params=_cparams(("arbitrary", "arbitrary")),
        name="sample_attention",
    )(page_table, q, k_new, v_new, *([cache_kt] * npg), *([cache_vt] * npg))


def _split_bf16(x):
    hi = x.astype(BF16)
    return hi, (x - hi.astype(F32)).astype(BF16)


def _post_kernel(att_ref, pz_ref, pre_ref, gab_ref, x_ref, gt1_ref, sh2_ref, sc2_ref,
                 wao_ref, wpool_ref, pscale_ref, wout_ref, gpost1_ref, gpre2_ref, wrh_ref, wrl_ref, br_ref,
                 x1_ref, h2_ref, topi_ref, gates_ref, *, pos0, zero_first_prefix):
    bb, tt, d = x_ref.shape
    rows = bb * tt
    i = pl.program_id(1)
    n_groups = len(POOL_WINDOWS)
    gc = pz_ref.shape[2] // n_groups

    pz = pz_ref[...]
    prefix = pre_ref[...]
    if zero_first_prefix:
        prefix = jnp.where(i == 0, 0.0, prefix)
    xp = jnp.concatenate([prefix, pz], axis=1)
    pos = pos0 + i * tt + lax.broadcasted_iota(jnp.int32, (1, tt, 1), 1)
    pbs = []
    for g, w in enumerate(POOL_WINDOWS):
        a = xp[:, :, g * gc:(g + 1) * gc]
        span = 1
        while span < w:
            a = a + pltpu.roll(a, span, 1)
            span *= 2
        win = a[:, POOL_PREFIX_ROWS:, :]
        cnt = jnp.minimum(w, pos + 1).astype(F32)
        pooled = win / cnt - pz[:, :, g * gc:(g + 1) * gc]
        pbs.append(jnp.dot(pooled.reshape(rows, gc).astype(BF16), wpool_ref[g], preferred_element_type=F32))
    pb = jnp.concatenate(pbs, axis=1) * pscale_ref[...]

    att = jnp.dot(att_ref[...].reshape(rows, att_ref.shape[2]).astype(BF16), wao_ref[...],
                  preferred_element_type=F32)
    gab = gab_ref[...].reshape(rows, 2 * d)
    mix = _sigmoid(gab[:, :d]) * att + _sigmoid(gab[:, d:]) * pb
    mix = jnp.dot(mix.astype(BF16), wout_ref[...], preferred_element_type=F32)
    x1 = x_ref[...] + gt1_ref[...] * _rms(mix, gpost1_ref[...]).reshape(bb, tt, d)
    x1_ref[...] = x1
    h2 = (_rms(x1, gpre2_ref[...]) * (1.0 + sc2_ref[...]) + sh2_ref[...]).reshape(rows, d)
    _store_tile_rows(h2_ref, h2)

    h_hi, h_lo = _split_bf16(h2)
    logits = (jnp.dot(h_hi, wrh_ref[...], preferred_element_type=F32)
              + (jnp.dot(h_lo, wrh_ref[...], preferred_element_type=F32)
                 + jnp.dot(h_hi, wrl_ref[...], preferred_element_type=F32))) + br_ref[...]
    n_exp = logits.shape[1]
    lane = lax.broadcasted_iota(jnp.int32, logits.shape, 1).astype(F32)
    kcol = lax.broadcasted_iota(jnp.int32, (rows, MOE_TOP_K), 1)
    vals = jnp.zeros((rows, MOE_TOP_K), F32)
    idxs = jnp.zeros((rows, MOE_TOP_K), F32)
    v = logits
    for k in range(MOE_TOP_K):
        mx = jnp.max(v, axis=-1, keepdims=True)
        ix = jnp.min(jnp.where(v == mx, lane, float(n_exp)), axis=-1, keepdims=True)
        vals = jnp.where(kcol == k, mx, vals)
        idxs = jnp.where(kcol == k, ix, idxs)
        v = jnp.where(lane == ix, -jnp.inf, v)
    e = jnp.exp(vals - vals[:, 0:1])
    gates = e / jnp.sum(e, axis=-1, keepdims=True)
    topi_ref[...] = idxs.astype(jnp.int32).reshape(bb, tt, MOE_TOP_K)
    gates_ref[...] = gates.reshape(bb, tt, MOE_TOP_K)


def _post(att, pz, prefix, gab, x, mod3, weights, bb, tt, pos0, zero_first_prefix):
    bx, t, d = x.shape
    aw, pw = att.shape[2], pz.shape[2]
    grid = (bx // bb, t // tt)
    row = lambda width: pl.BlockSpec((bb, tt, width), lambda b, i: (b, i, 0))
    modspec = lambda slot: pl.BlockSpec((bb, 1, d), lambda b, i: (b, 0, slot))
    full = lambda a: pl.BlockSpec(a.shape, lambda b, i: (0,) * a.ndim)
    if zero_first_prefix:
        per_tile = tt // POOL_PREFIX_ROWS
        pre_spec = pl.BlockSpec((bb, POOL_PREFIX_ROWS, pw), lambda b, i: (b, jnp.maximum(i * per_tile - 1, 0), 0))
    else:
        assert t == tt
        pre_spec = pl.BlockSpec((bb, POOL_PREFIX_ROWS, pw), lambda b, i: (b, 0, 0))
    out = lambda width, dt: jax.ShapeDtypeStruct((bx, t, width), dt)
    return pl.pallas_call(
        functools.partial(_post_kernel, pos0=pos0, zero_first_prefix=zero_first_prefix),
        grid=grid,
        in_specs=[row(aw), row(pw), pre_spec, row(2 * d), row(d), modspec(2), modspec(3), modspec(4)]
                 + [full(w) for w in weights],
        out_specs=[row(d), pl.BlockSpec((bb * tt * SUBLANES, LANES), lambda b, i: (b * grid[1] + i, 0)),
                   row(MOE_TOP_K), row(MOE_TOP_K)],
        out_shape=[out(d, F32), jax.ShapeDtypeStruct((bx * t * SUBLANES, LANES), F32),
                   out(MOE_TOP_K, jnp.int32), out(MOE_TOP_K, F32)],
        compiler_params=_cparams(("arbitrary", "arbitrary")),
        name="post",
    )(att, pz, prefix, gab, x, mod3, mod3, mod3, *weights)


def _plan_kernel(topi_ref, rank_ref, counts_ref, base_ref):
    i = pl.program_id(0)
    tt = topi_ref.shape[0]
    n_exp = counts_ref.shape[1]

    @pl.when(i == 0)
    def _init():
        base_ref[...] = jnp.zeros(base_ref.shape, F32)

    topi = topi_ref[...]
    lane = lax.broadcasted_iota(jnp.int32, (tt, n_exp), 1)
    onehots = [jnp.where(topi[:, k:k + 1] == lane, 1.0, 0.0) for k in range(MOE_TOP_K)]
    tot = onehots[0]
    for oh in onehots[1:]:
        tot = tot + oh
    tri = jnp.where(lax.broadcasted_iota(jnp.int32, (tt, tt), 0) > lax.broadcasted_iota(jnp.int32, (tt, tt), 1),
                    1.0, 0.0).astype(BF16)
    before = jnp.dot(tri, tot.astype(BF16), preferred_element_type=F32) + base_ref[...]
    kcol = lax.broadcasted_iota(jnp.int32, (tt, MOE_TOP_K), 1)
    rank = jnp.zeros((tt, MOE_TOP_K), F32)
    for k in range(MOE_TOP_K):
        rank = jnp.where(kcol == k, jnp.sum(onehots[k] * before, axis=-1, keepdims=True), rank)
    rank_ref[...] = rank.astype(jnp.int32)
    base_ref[...] = base_ref[...] + jnp.sum(tot, axis=0, keepdims=True)
    counts_ref[...] = base_ref[...].astype(jnp.int32)


def _moe_plan(topi, n_exp):
    t = topi.shape[0]
    tt = PLAN_ROWS
    assert t % tt == 0
    return pl.pallas_call(
        _plan_kernel,
        grid=(t // tt,),
        in_specs=[pl.BlockSpec((tt, MOE_TOP_K), lambda i: (i, 0))],
        out_specs=[pl.BlockSpec((tt, MOE_TOP_K), lambda i: (i, 0)),
                   pl.BlockSpec((1, n_exp), lambda i: (0, 0))],
        out_shape=[jax.ShapeDtypeStruct((t, MOE_TOP_K), jnp.int32),
                   jax.ShapeDtypeStruct((1, n_exp), jnp.int32)],
        scratch_shapes=[pltpu.VMEM((1, n_exp), F32)],
        compiler_params=_cparams(("arbitrary",)),
        name="moe_plan",
    )(topi)


def _dispatch_kernel(last_blk_ref, dest_ref, h_ref, xs_ref, zero_ref, sem, zsem):
    i = pl.program_id(0)
    tile_rows = h_ref.shape[0]
    tt = tile_rows // SUBLANES
    n_fill = last_blk_ref.shape[0]

    @pl.when(i == 0)
    def _zero_padding():
        zero_ref[...] = jnp.zeros(zero_ref.shape, F32)

        def fill(e):
            start = pl.multiple_of(jnp.maximum(last_blk_ref[e], 0), SUBLANES)
            return pltpu.make_async_copy(zero_ref, xs_ref.at[pl.ds(start, zero_ref.shape[0]), :], zsem)

        for e in range(n_fill):
            @pl.when(last_blk_ref[e] >= 0)
            def _start(e=e):
                fill(e).start()
        for e in range(n_fill):
            @pl.when(last_blk_ref[e] >= 0)
            def _wait(e=e):
                fill(e).wait()

    def row_copy(r, k):
        dst = pl.multiple_of(dest_ref[0, 0, r * MOE_TOP_K + k], SUBLANES)
        return pltpu.make_async_copy(h_ref.at[pl.ds(pl.multiple_of(r * SUBLANES, SUBLANES), SUBLANES), :],
                                     xs_ref.at[pl.ds(dst, SUBLANES), :], sem)

    def issue(r, carry):
        for k in range(MOE_TOP_K):
            row_copy(r, k).start()
        return carry

    lax.fori_loop(0, tt, issue, 0, unroll=ISSUE_UNROLL)
    for k in range(MOE_TOP_K):
        pltpu.make_async_copy(h_ref, xs_ref.at[pl.ds(0, tile_rows), :], sem).wait()


def _moe_dispatch(last_blk, dest3, h3_all, p_rows, bm):
    tile_rows = DISPATCH_ROWS * SUBLANES
    assert h3_all.shape[0] % tile_rows == 0
    grid_spec = pltpu.PrefetchScalarGridSpec(
        num_scalar_prefetch=1,
        grid=(h3_all.shape[0] // tile_rows,),
        in_specs=[pl.BlockSpec((1, 1, DISPATCH_ROWS * MOE_TOP_K), lambda i, lb: (i, 0, 0), memory_space=pltpu.SMEM),
                  pl.BlockSpec((tile_rows, LANES), lambda i, lb: (i, 0))],
        out_specs=pl.BlockSpec(memory_space=pl.ANY),
        scratch_shapes=[pltpu.VMEM((bm * SUBLANES, LANES), F32),
                        pltpu.SemaphoreType.DMA(()), pltpu.SemaphoreType.DMA(())],
    )
    return pl.pallas_call(
        _dispatch_kernel,
        grid_spec=grid_spec,
        out_shape=jax.ShapeDtypeStruct((p_rows * SUBLANES, LANES), F32),
        compiler_params=_cparams(("arbitrary",)),
        name="moe_dispatch",
    )(last_blk, dest3, h3_all)


def _expert_kernel(blk_e_ref, blk_first_ref, n_used_ref, xs_ref, wgu_ref, bgu_ref, wd_ref, bd_ref, ys_ref,
                   wgu_bf_ref, wd_bf_ref):
    del blk_e_ref
    i = pl.program_id(0)
    dff = wd_ref.shape[1]
    bm = xs_ref.shape[0] // SUBLANES
    used = i < n_used_ref[0]

    @pl.when(used & (blk_first_ref[i] == 1))
    def _cast_weights():
        wgu_bf_ref[...] = wgu_ref[0].astype(BF16)
        wd_bf_ref[...] = wd_ref[0].astype(BF16)

    @pl.when(used)
    def _compute():
        x = _load_tile_rows(xs_ref, bm).astype(BF16)
        gu = jnp.dot(x, wgu_bf_ref[...], preferred_element_type=F32) + bgu_ref[0]
        gt = jnp.minimum(gu[:, :dff], SWIGLU_LIMIT)
        up = jnp.clip(gu[:, dff:], -SWIGLU_LIMIT, SWIGLU_LIMIT)
        act = (up + 1.0) * (gt * _sigmoid(gt * SWIGLU_ALPHA))
        _store_tile_rows(ys_ref, jnp.dot(act.astype(BF16), wd_bf_ref[...], preferred_element_type=F32) + bd_ref[0])

    @pl.when(jnp.logical_not(used))
    def _skip():
        ys_ref[...] = jnp.zeros(ys_ref.shape, F32)


def _moe_experts(blk_e, blk_first, n_used, xs3, w_gu, b_gu3, w_down, b_down3):
    bm = EXPERT_ROWS
    n_exp, d, two_f = w_gu.shape
    dff = w_down.shape[1]
    assert d == SUBLANES * LANES
    tile_rows = bm * SUBLANES
    grid_spec = pltpu.PrefetchScalarGridSpec(
        num_scalar_prefetch=3,
        grid=(xs3.shape[0] // tile_rows,),
        in_specs=[pl.BlockSpec((tile_rows, LANES), lambda i, be, bf, nu: (i, 0)),
                  pl.BlockSpec((1, d, two_f), lambda i, be, bf, nu: (be[i], 0, 0)),
                  pl.BlockSpec((1, 1, two_f), lambda i, be, bf, nu: (be[i], 0, 0)),
                  pl.BlockSpec((1, dff, d), lambda i, be, bf, nu: (be[i], 0, 0)),
                  pl.BlockSpec((1, 1, d), lambda i, be, bf, nu: (be[i], 0, 0))],
        out_specs=pl.BlockSpec((tile_rows, LANES), lambda i, be, bf, nu: (i, 0)),
        scratch_shapes=[pltpu.VMEM((d, two_f), BF16), pltpu.VMEM((dff, d), BF16)],
    )
    return pl.pallas_call(
        _expert_kernel,
        grid_spec=grid_spec,
        out_shape=jax.ShapeDtypeStruct(xs3.shape, F32),
        compiler_params=_cparams(("arbitrary",)),
        name="moe_experts",
    )(blk_e, blk_first, n_used, xs3, w_gu, b_gu3, w_down, b_down3)


def _combine_kernel(dest_ref, gates_ref, x1_ref, gt2_ref, gpost2_ref, ys_ref, y_ref, buf_ref, sem):
    bb, tt, d = x1_ref.shape
    rows = bb * tt

    def row_copy(r, k):
        src = pl.multiple_of(dest_ref[0, 0, r * MOE_TOP_K + k], SUBLANES)
        return pltpu.make_async_copy(ys_ref.at[pl.ds(src, SUBLANES), :],
                                     buf_ref.at[k, pl.ds(pl.multiple_of(r * SUBLANES, SUBLANES), SUBLANES), :], sem)

    def issue(r, carry):
        for k in range(MOE_TOP_K):
            row_copy(r, k).start()
        return carry

    lax.fori_loop(0, rows, issue, 0, unroll=ISSUE_UNROLL)
    for k in range(MOE_TOP_K):
        pltpu.make_async_copy(ys_ref.at[pl.ds(0, rows * SUBLANES), :], buf_ref.at[k], sem).wait()

    gates = gates_ref[...].reshape(rows, MOE_TOP_K)
    ff = gates[:, 0:1] * _load_tile_rows(buf_ref, rows, lead=(0,))
    for k in range(1, MOE_TOP_K):
        ff = ff + gates[:, k:k + 1] * _load_tile_rows(buf_ref, rows, lead=(k,))
    y_ref[...] = x1_ref[...] + gt2_ref[...] * _rms(ff, gpost2_ref[...]).reshape(bb, tt, d)


def _moe_combine(dest3, gates, x1, mod3, g_post2, ys, bb, tt):
    bx, t, d = x1.shape
    rows = bb * tt
    n_t = t // tt
    row = lambda width: pl.BlockSpec((bb, tt, width), lambda b, i: (b, i, 0))
    return pl.pallas_call(
        _combine_kernel,
        grid=(bx // bb, n_t),
        in_specs=[pl.BlockSpec((1, 1, rows * MOE_TOP_K), lambda b, i: (b * n_t + i, 0, 0), memory_space=pltpu.SMEM),
                  row(MOE_TOP_K), row(d),
                  pl.BlockSpec((bb, 1, d), lambda b, i: (b, 0, 5)),
                  pl.BlockSpec((1, d), lambda b, i: (0, 0)),
                  pl.BlockSpec(memory_space=pl.ANY)],
        out_specs=row(d),
        out_shape=jax.ShapeDtypeStruct((bx, t, d), F32),
        scratch_shapes=[pltpu.VMEM((MOE_TOP_K, rows * SUBLANES, LANES), F32), pltpu.SemaphoreType.DMA(())],
        compiler_params=_cparams(("arbitrary", "arbitrary")),
        name="moe_combine",
    )(dest3, gates, x1, mod3, g_post2, ys)


def _layer(xp, xs, cp, cs, cache_kt, cache_vt, state_pool_l, page_table, layer, p):
    b, s_len, d = xp.shape
    n, t_new, _ = xs.shape
    n_heads, hd, page = cache_kt.shape[2:]
    aw = n_heads * hd
    pw = state_pool_l.shape[2]
    n_exp = p['w_router'].shape[1]
    sb = min(SAMPLE_SEQS, n)
    pr = min(PROMPT_ROWS, s_len)

    mod = _mod(jnp.concatenate([cp, cs], axis=0), p['w_mod'], p['b_mod'])
    mod_p = mod[:b].reshape(b, 1, N_MOD * d)
    mod_s = mod[b:].reshape(n, 1, N_MOD * d)

    w_in_bf = p['w_in'].astype(BF16)
    w_kvt_bf = p['w_in'][:, aw:3 * aw].T.astype(BF16)
    qp, kp, ktp, vtp, vtbp, pzp, gabp = _in_proj_prompt(xp, mod_p, p['g_pre1'], w_in_bf, w_kvt_bf, aw, pw, pr)
    qs, ks, vs, pzs, gabs = _in_proj_sample(xs, mod_s, p['g_pre1'], w_in_bf, aw, pw, sb)

    att_p = _prompt_attention(qp, kp, vtbp, n_heads, hd)
    att_s = _sample_attention(qs, ks, vs, cache_kt, cache_vt, page_table, layer)

    wr_hi = p['w_router'].astype(BF16)
    wr_lo = (p['w_router'] - wr_hi.astype(F32)).astype(BF16)
    post_w = (p['w_attn_o'].astype(BF16), p['w_pool'].astype(BF16), p['pool_scale'], p['w_out'].astype(BF16),
              p['g_post1'], p['g_pre2'], wr_hi, wr_lo, p['b_router'])
    x1p, h2p, tip, gp = _post(att_p, pzp, pzp, gabp, xp, mod_p, post_w, 1, pr, 0, True)
    state16 = jnp.concatenate([jnp.zeros((n, 1, pw), F32), state_pool_l], axis=1)
    x1s, h2s, tis, gs = _post(att_s, pzs, state16, gabs, xs, mod_s, post_w, sb, t_new,
                              page_table.shape[1] * page, False)

    tp, ts = b * s_len, n * t_new
    h3_all = jnp.concatenate([h2p, h2s], axis=0)
    topi = jnp.concatenate([tip.reshape(tp, MOE_TOP_K), tis.reshape(ts, MOE_TOP_K)], axis=0)
    t_all = tp + ts
    rank, counts = _moe_plan(topi, n_exp)
    bm = EXPERT_ROWS
    counts = counts[0]
    padded = (counts + bm - 1) // bm * bm
    pends = jnp.cumsum(padded)
    pstarts = pends - padded
    n_blocks = (t_all * MOE_TOP_K + n_exp * (bm - 1) + bm - 1) // bm
    blk_start = jnp.arange(n_blocks, dtype=jnp.int32) * bm
    blk_e = jnp.minimum(jnp.sum((pends[None, :] <= blk_start[:, None]).astype(jnp.int32), axis=1), n_exp - 1)
    blk_first = jnp.concatenate([jnp.ones((1,), jnp.int32), (blk_e[1:] != blk_e[:-1]).astype(jnp.int32)])
    n_used = (pends[-1:] // bm).astype(jnp.int32)
    tail_blk = n_used + jnp.arange(n_exp, dtype=jnp.int32)
    last_blk = jnp.concatenate([jnp.where(padded > 0, (pends - bm) * SUBLANES, -1),
                                jnp.where(tail_blk < n_blocks, tail_blk * (bm * SUBLANES), -1)]).astype(jnp.int32)
    onehot_e = topi[:, :, None] == jnp.arange(n_exp, dtype=jnp.int32)
    dest = (jnp.sum(jnp.where(onehot_e, pstarts.astype(jnp.int32), 0), axis=-1) + rank) * SUBLANES

    xs_sorted = _moe_dispatch(last_blk, dest.reshape(t_all // DISPATCH_ROWS, 1, DISPATCH_ROWS * MOE_TOP_K),
                              h3_all, n_blocks * bm, bm)
    ys = _moe_experts(blk_e, blk_first, n_used, xs_sorted, p['w_gu'], p['b_gu'][:, None, :],
                      p['w_down'], p['b_down'][:, None, :])

    cr = min(COMBINE_ROWS, s_len)
    csb = min(max(COMBINE_ROWS // t_new, 1), n)
    dest_p = dest[:tp].reshape(tp // cr, 1, cr * MOE_TOP_K)
    dest_s = dest[tp:].reshape(ts // (csb * t_new), 1, csb * t_new * MOE_TOP_K)
    yp = _moe_combine(dest_p, gp, x1p, mod_p, p['g_post2'], ys, 1, cr)
    ysm = _moe_combine(dest_s, gs, x1s, mod_s, p['g_post2'], ys, csb, t_new)

    k_out = ktp.reshape(b, n_heads, hd, s_len).transpose(0, 3, 1, 2)
    v_out = vtp.reshape(b, n_heads, hd, s_len).transpose(0, 3, 1, 2)
    pool_p = pzp[:, s_len - (POOL_PREFIX_ROWS - 1):, :]
    pool_s = jnp.concatenate([state_pool_l, pzs], axis=1)[:, -(POOL_PREFIX_ROWS - 1):, :]
    return (yp, ysm, k_out, v_out, pool_p,
            ks.reshape(n, t_new, n_heads, hd), vs.reshape(n, t_new, n_heads, hd), pool_s)


def kernel(x_prompt, x_sample, c_prompt, c_sample, cache_k, cache_v, state_pool, page_table, w_mod, b_mod, g_pre1, g_post1, g_pre2, g_post2, w_in, w_attn_o, w_pool, pool_scale, w_out, w_router, b_router, w_gu, b_gu, w_down, b_down):
    depth = w_mod.shape[0]
    hp, hs = x_prompt, x_sample
    cache_kt = cache_k.transpose(0, 1, 3, 4, 2)
    cache_vt = cache_v.transpose(0, 1, 3, 4, 2)
    outs = [[] for _ in range(6)]
    for l in range(depth):
        p = {'w_mod': w_mod[l], 'b_mod': b_mod[l][None, :], 'g_pre1': g_pre1[l][None, :],
             'g_post1': g_post1[l][None, :], 'g_pre2': g_pre2[l][None, :], 'g_post2': g_post2[l][None, :],
             'w_in': w_in[l], 'w_attn_o': w_attn_o[l], 'w_pool': w_pool[l], 'pool_scale': pool_scale[l][None, :],
             'w_out': w_out[l], 'w_router': w_router[l], 'b_router': b_router[l][None, :],
             'w_gu': w_gu[l], 'b_gu': b_gu[l], 'w_down': w_down[l], 'b_down': b_down[l]}
        hp, hs, kp, vp, pp, ks, vs, ps = _layer(hp, hs, c_prompt, c_sample, cache_kt, cache_vt, state_pool[l],
                                                page_table, l, p)
        for lst, val in zip(outs, (kp, vp, pp, ks, vs, ps)):
            lst.append(val)
    return (hp, hs) + tuple(jnp.stack(o) for o in outs)
```

```python
import functools

import jax
import jax.numpy as jnp
from jax import lax
from jax.experimental import pallas as pl
from jax.experimental.pallas import tpu as pltpu

F32 = jnp.float32
BF16 = jnp.bfloat16
HIGHEST = lax.Precision.HIGHEST

MOBA_BLOCK = 256
MOBA_TOPK = 3
POOL_WINDOWS = (2, 4, 8, 16)
POOL_PREFIX_ROWS = 16
MOE_TOP_K = 4
SWIGLU_LIMIT = 7.0
SWIGLU_ALPHA = 1.702
N_MOD = 6
NORM_EPS = 1e-6
MASK_BIAS = -1e30
LOG2_E = 1.4426950408889634

LANES = 128
SUBLANES = 8
VMEM_LIMIT = 56 * 1024 * 1024
PROMPT_ROWS = 512
SAMPLE_SEQS = 16
PAGES_PER_STEP = 8
PLAN_ROWS = 512
DISPATCH_ROWS = 512
EXPERT_ROWS = 512
COMBINE_ROWS = 256
ISSUE_UNROLL = 4


def _cparams(sem):
    return pltpu.CompilerParams(dimension_semantics=sem, vmem_limit_bytes=VMEM_LIMIT)


def _rms(x, g):
    return x * lax.rsqrt(jnp.mean(x * x, axis=-1, keepdims=True) + NORM_EPS) * g


def _dot_nt(a, b, precision=None):
    return lax.dot_general(a, b, (((1,), (1,)), ((), ())), preferred_element_type=F32, precision=precision)


def _sigmoid(x):
    return 0.5 * jnp.tanh(0.5 * x) + 0.5


def _load_tile_rows(ref, rows, lead=()):
    return jnp.concatenate([ref[lead + (pl.ds(c, rows, stride=SUBLANES), slice(None))] for c in range(SUBLANES)],
                           axis=1)


def _store_tile_rows(ref, x):
    rows = x.shape[0]
    for c in range(SUBLANES):
        ref[pl.ds(c, rows, stride=SUBLANES), :] = x[:, c * LANES:(c + 1) * LANES]


def _mod_kernel(c_ref, w_ref, b_ref, o_ref):
    c = c_ref[...]
    s = c * jax.nn.sigmoid(c)
    o_ref[...] = jnp.dot(s, w_ref[...], preferred_element_type=F32, precision=HIGHEST) + b_ref[...]


def _mod(c_all, w_mod, b_mod):
    n, d = c_all.shape
    return pl.pallas_call(
        _mod_kernel,
        grid=(N_MOD,),
        in_specs=[pl.BlockSpec((n, d), lambda j: (0, 0)),
                  pl.BlockSpec((d, d), lambda j: (0, j)),
                  pl.BlockSpec((1, d), lambda j: (0, j))],
        out_specs=pl.BlockSpec((n, d), lambda j: (0, j)),
        out_shape=jax.ShapeDtypeStruct((n, N_MOD * d), F32),
        compiler_params=_cparams(("arbitrary",)),
        name="mod",
    )(c_all, w_mod, b_mod)


def _modulated(x_ref, sh_ref, sc_ref, g_ref):
    bb, tt, d = x_ref.shape
    h = _rms(x_ref[...], g_ref[...]) * (1.0 + sc_ref[...]) + sh_ref[...]
    return h.reshape(bb * tt, d).astype(BF16)


def _in_proj_sample_kernel(x_ref, sh_ref, sc_ref, g_ref, w_ref, q_ref, k_ref, v_ref, pz_ref, gab_ref, *, aw, pw):
    bb, tt, d = x_ref.shape
    hb = _modulated(x_ref, sh_ref, sc_ref, g_ref)
    col = 0
    for ref, width in ((q_ref, aw), (k_ref, aw), (v_ref, aw), (pz_ref, pw), (gab_ref, 2 * d)):
        u = jnp.dot(hb, w_ref[:, col:col + width], preferred_element_type=F32)
        ref[...] = u.reshape(bb, tt, width)
        col += width


def _in_proj_prompt_kernel(x_ref, sh_ref, sc_ref, g_ref, w_ref, wkvt_ref,
                           q_ref, k_ref, kt_ref, vt_ref, vtb_ref, pz_ref, gab_ref, *, aw, pw, qscale):
    _, tt, d = x_ref.shape
    hb = _modulated(x_ref, sh_ref, sc_ref, g_ref)
    q_ref[0] = (jnp.dot(hb, w_ref[:, 0:aw], preferred_element_type=F32) * qscale).astype(BF16)
    k_ref[0] = jnp.dot(hb, w_ref[:, aw:2 * aw], preferred_element_type=F32).astype(BF16)
    col = 3 * aw
    pz_ref[0] = jnp.dot(hb, w_ref[:, col:col + pw], preferred_element_type=F32)
    gab_ref[0] = jnp.dot(hb, w_ref[:, col + pw:col + pw + 2 * d], preferred_element_type=F32).astype(BF16)
    kvt = _dot_nt(wkvt_ref[...], hb)
    kt_ref[0] = kvt[0:aw]
    vt_ref[0] = kvt[aw:2 * aw]
    for j in range(tt // MOBA_BLOCK):
        vtb_ref[0, j] = kvt[aw:2 * aw, j * MOBA_BLOCK:(j + 1) * MOBA_BLOCK].astype(BF16)


def _in_proj_sample(x, mod3, g_pre1, w_in_bf, aw, pw, bb):
    bx, t, d = x.shape
    row = lambda width: pl.BlockSpec((bb, t, width), lambda b: (b, 0, 0))
    modspec = lambda slot: pl.BlockSpec((bb, 1, d), lambda b: (b, 0, slot))
    out = lambda width: jax.ShapeDtypeStruct((bx, t, width), F32)
    return pl.pallas_call(
        functools.partial(_in_proj_sample_kernel, aw=aw, pw=pw),
        grid=(bx // bb,),
        in_specs=[row(d), modspec(0), modspec(1),
                  pl.BlockSpec((1, d), lambda b: (0, 0)),
                  pl.BlockSpec(w_in_bf.shape, lambda b: (0, 0))],
        out_specs=[row(aw), row(aw), row(aw), row(pw), row(2 * d)],
        out_shape=[out(aw), out(aw), out(aw), out(pw), out(2 * d)],
        compiler_params=_cparams(("arbitrary",)),
        name="in_proj_sample",
    )(x, mod3, mod3, g_pre1, w_in_bf)


def _in_proj_prompt(x, mod3, g_pre1, w_in_bf, w_kvt_bf, aw, pw, tt, qscale):
    bx, t, d = x.shape
    assert tt % MOBA_BLOCK == 0 and t % tt == 0
    nb, per = t // MOBA_BLOCK, tt // MOBA_BLOCK
    row = lambda width: pl.BlockSpec((1, tt, width), lambda b, i: (b, i, 0))
    col = pl.BlockSpec((1, aw, tt), lambda b, i: (b, 0, i))
    modspec = lambda slot: pl.BlockSpec((1, 1, d), lambda b, i: (b, 0, slot))
    return pl.pallas_call(
        functools.partial(_in_proj_prompt_kernel, aw=aw, pw=pw, qscale=qscale),
        grid=(bx, t // tt),
        in_specs=[row(d), modspec(0), modspec(1),
                  pl.BlockSpec((1, d), lambda b, i: (0, 0)),
                  pl.BlockSpec(w_in_bf.shape, lambda b, i: (0, 0)),
                  pl.BlockSpec(w_kvt_bf.shape, lambda b, i: (0, 0))],
        out_specs=[row(aw), row(aw), col, col,
                   pl.BlockSpec((1, per, aw, MOBA_BLOCK), lambda b, i: (b, i, 0, 0)),
                   row(pw), row(2 * d)],
        out_shape=[jax.ShapeDtypeStruct((bx, t, aw), BF16), jax.ShapeDtypeStruct((bx, t, aw), BF16),
                   jax.ShapeDtypeStruct((bx, aw, t), F32), jax.ShapeDtypeStruct((bx, aw, t), F32),
                   jax.ShapeDtypeStruct((bx, nb, aw, MOBA_BLOCK), BF16),
                   jax.ShapeDtypeStruct((bx, t, pw), F32), jax.ShapeDtypeStruct((bx, t, 2 * d), BF16)],
        compiler_params=_cparams(("arbitrary", "arbitrary")),
        name="in_proj_prompt",
    )(x, mod3, mod3, g_pre1, w_in_bf, w_kvt_bf)


def _moba_select(gate, n_past, n_cand, axis):
    slot = lax.broadcasted_iota(jnp.int32, gate.shape, axis)
    past = slot < n_past
    g = jnp.where(past, gate, -jnp.inf)
    rank = jnp.zeros(gate.shape, jnp.int32)
    for j in range(n_cand):
        other = g[:, j:j + 1] if axis == 1 else g[j:j + 1, :]
        beats = jnp.where(other > g, 1, jnp.where(other == g, jnp.where(j < slot, 1, 0), 0))
        rank = rank + jnp.where(j < n_past, beats, 0)
    return past & (rank < MOBA_TOPK)


def _prompt_attn_kernel(q_ref, k_ref, vtb_ref, o_ref, kaug_ref, kmean_ref, qaug_ref, *, n_heads, hd):
    i = pl.program_id(1)
    s_len = k_ref.shape[1]
    tq = q_ref.shape[1]
    nb = s_len // MOBA_BLOCK
    nbp = kmean_ref.shape[1]
    slots = kaug_ref.shape[2] - hd

    @pl.when(i == 0)
    def _prepare():
        kf = k_ref[0].astype(F32)
        key_blk = lax.broadcasted_iota(jnp.int32, (s_len, slots), 0) // MOBA_BLOCK
        onehot = jnp.where(key_blk == lax.broadcasted_iota(jnp.int32, (s_len, slots), 1), 1.0, 0.0).astype(BF16)
        kmean = kf.reshape(nb, MOBA_BLOCK, kf.shape[1]).sum(axis=1) * (1.0 / MOBA_BLOCK)
        kmean_ref[...] = jnp.zeros(kmean_ref.shape, F32)
        for h in range(n_heads):
            kaug_ref[h, :, 0:hd] = k_ref[0, :, h * hd:(h + 1) * hd]
            kaug_ref[h, :, hd:hd + slots] = onehot
            kmean_ref[h, 0:nb, :] = kmean[:, h * hd:(h + 1) * hd]

    qt = q_ref[0].astype(F32).T
    slot = lax.broadcasted_iota(jnp.int32, (nbp, tq), 0)
    gates = [jnp.dot(kmean_ref[h], qt[h * hd:(h + 1) * hd], preferred_element_type=F32, precision=HIGHEST)
             for h in range(n_heads)]
    for h in range(n_heads):
        sel = _moba_select(gates[h], i, nb, axis=0) | (slot == i)
        bias = jnp.where(sel, 0.0, MASK_BIAS)
        parts = [qt[h * hd:(h + 1) * hd], bias]
        if slots > nbp:
            parts.append(jnp.zeros((slots - nbp, tq), F32))
        qaug_ref[h] = jnp.concatenate(parts, axis=0).astype(BF16)

    def scores(h, r0):
        return jnp.dot(kaug_ref[h, pl.ds(r0, MOBA_BLOCK), :], qaug_ref[h], preferred_element_type=F32)

    key_id = lax.broadcasted_iota(jnp.int32, (MOBA_BLOCK, tq), 0)
    qry_id = lax.broadcasted_iota(jnp.int32, (MOBA_BLOCK, tq), 1)
    diag0 = pl.multiple_of(i * MOBA_BLOCK, MOBA_BLOCK)
    ss = [scores(h, diag0) for h in range(n_heads)]
    soft = []
    for h in range(n_heads):
        s = jnp.where(key_id <= qry_id, ss[h], -jnp.inf)
        m = jnp.max(s, axis=0, keepdims=True)
        p = jnp.exp2(s - m)
        soft.append((m, jnp.sum(p, axis=0, keepdims=True), p.astype(BF16)))
    state = [(m, l, jnp.dot(vtb_ref[0, i, h * hd:(h + 1) * hd, :], p, preferred_element_type=F32))
             for h, (m, l, p) in enumerate(soft)]

    def body(j, state):
        r0 = pl.multiple_of(j * MOBA_BLOCK, MOBA_BLOCK)
        ss = [scores(h, r0) for h in range(n_heads)]
        soft = []
        for h in range(n_heads):
            m_old, l_old, _ = state[h]
            m_new = jnp.maximum(m_old, jnp.max(ss[h], axis=0, keepdims=True))
            alpha = jnp.exp2(m_old - m_new)
            p = jnp.exp2(ss[h] - m_new)
            soft.append((m_new, alpha, alpha * l_old + jnp.sum(p, axis=0, keepdims=True), p.astype(BF16)))
        new_state = []
        for h in range(n_heads):
            m_new, alpha, l_new, p = soft[h]
            acc = alpha * state[h][2] + jnp.dot(vtb_ref[0, j, h * hd:(h + 1) * hd, :], p,
                                                preferred_element_type=F32)
            new_state.append((m_new, l_new, acc))
        return tuple(new_state)

    state = lax.fori_loop(0, i, body, tuple(state))
    o_ref[0] = jnp.concatenate([acc / l for _, l, acc in state], axis=0).T.astype(BF16)


def _prompt_attention(q_bf, k_bf, vtb, n_heads, hd):
    b, s_len, aw = q_bf.shape
    assert s_len % MOBA_BLOCK == 0
    nb = s_len // MOBA_BLOCK
    nbp = -(-nb // SUBLANES) * SUBLANES
    slots = LANES - hd
    assert nbp <= slots
    tq = MOBA_BLOCK
    return pl.pallas_call(
        functools.partial(_prompt_attn_kernel, n_heads=n_heads, hd=hd),
        grid=(b, s_len // tq),
        in_specs=[pl.BlockSpec((1, tq, aw), lambda bi, i: (bi, i, 0)),
                  pl.BlockSpec((1, s_len, aw), lambda bi, i: (bi, 0, 0)),
                  pl.BlockSpec((1, nb, aw, MOBA_BLOCK), lambda bi, i: (bi, 0, 0, 0))],
        out_specs=pl.BlockSpec((1, tq, aw), lambda bi, i: (bi, i, 0)),
        out_shape=jax.ShapeDtypeStruct((b, s_len, aw), BF16),
        scratch_shapes=[pltpu.VMEM((n_heads, s_len, hd + slots), BF16),
                        pltpu.VMEM((n_heads, nbp, hd), F32),
                        pltpu.VMEM((n_heads, hd + slots, tq), BF16)],
        compiler_params=_cparams(("arbitrary", "arbitrary")),
        name="prompt_attention",
    )(q_bf, k_bf, vtb)


def _head_diag(o, n_heads, hd, rows_per_head):
    row_head = lax.broadcasted_iota(jnp.int32, (o.shape[0], hd), 0) // rows_per_head
    out = jnp.zeros((o.shape[0], hd), F32)
    for h in range(n_heads):
        out = out + jnp.where(row_head == h, o[:, h * hd:(h + 1) * hd], 0.0)
    return out


def _sample_attn_kernel(pt_ref, q_ref, kn_ref, vn_ref, *rest, n_heads, hd, scale):
    del pt_ref
    npg = PAGES_PER_STEP
    k_pages = rest[:npg]
    v_pages = rest[npg:2 * npg]
    o_ref = rest[2 * npg]
    m_ref, l_ref, ksum_ref, acc_ref = rest[2 * npg + 1:]
    step = pl.program_id(1)
    n_steps = pl.num_programs(1)
    t_new = q_ref.shape[1]
    aw = n_heads * hd
    rows = n_heads * t_new
    page = k_pages[0].shape[4]
    pages_per_blk = MOBA_BLOCK // page
    n_past = acc_ref.shape[0]

    @pl.when(step == 0)
    def _init():
        m_ref[...] = jnp.zeros(m_ref.shape, F32)
        l_ref[...] = jnp.zeros(l_ref.shape, F32)
        ksum_ref[...] = jnp.zeros(ksum_ref.shape, F32)

    q_rep = jnp.concatenate([q_ref[0]] * n_heads, axis=0)
    own_head = (lax.broadcasted_iota(jnp.int32, (rows, aw), 0) // t_new
                == lax.broadcasted_iota(jnp.int32, (rows, aw), 1) // hd)
    q_bd = jnp.where(own_head, q_rep, 0.0)
    q_bd_bf = (q_bd * scale).astype(BF16)
    blk_lane = lax.broadcasted_iota(jnp.int32, (1, LANES), 1)
    ones_bf = jnp.ones((MOBA_BLOCK, LANES), BF16)

    n_blk = npg // pages_per_blk
    scored = []
    for bi in range(n_blk):
        pr = range(bi * pages_per_blk, (bi + 1) * pages_per_blk)
        kt_bf = jnp.concatenate([k_pages[r][0, 0].reshape(aw, page) for r in pr], axis=1).astype(BF16)
        scored.append((jnp.dot(q_bd_bf, kt_bf, preferred_element_type=F32),
                       jnp.dot(kt_bf, ones_bf, preferred_element_type=F32)))
    soft = []
    for s, _ in scored:
        m = jnp.max(s, axis=-1, keepdims=True)
        p = jnp.exp(s - m)
        soft.append((m, jnp.sum(p, axis=-1, keepdims=True), p.astype(BF16)))
    for bi in range(n_blk):
        pr = range(bi * pages_per_blk, (bi + 1) * pages_per_blk)
        vt_bf = jnp.concatenate([v_pages[r][0, 0].reshape(aw, page) for r in pr], axis=1).astype(BF16)
        m, l, p = soft[bi]
        o = _dot_nt(p, vt_bf)
        j = step * n_blk + bi
        here = blk_lane == j
        m_ref[...] = jnp.where(here, m, m_ref[...])
        l_ref[...] = jnp.where(here, l, l_ref[...])
        ksum_ref[...] = jnp.where(here, scored[bi][1], ksum_ref[...])
        acc_ref[j] = _head_diag(o, n_heads, hd, t_new)

    @pl.when(step == n_steps - 1)
    def _finish():
        s = _dot_nt(q_bd * scale, kn_ref[0])
        causal = (lax.broadcasted_iota(jnp.int32, (rows, t_new), 1)
                  <= lax.broadcasted_iota(jnp.int32, (rows, t_new), 0) % t_new)
        s = jnp.where(causal, s, -jnp.inf)
        m_own = jnp.max(s, axis=-1, keepdims=True)
        p = jnp.exp(s - m_own)
        l_own = jnp.sum(p, axis=-1, keepdims=True)
        o_own = _head_diag(jnp.dot(p, vn_ref[0], preferred_element_type=F32), n_heads, hd, t_new)

        gate = jnp.dot(q_bd, ksum_ref[...] * (1.0 / MOBA_BLOCK), preferred_element_type=F32, precision=HIGHEST)
        sel = _moba_select(gate, n_past, n_past, axis=1)
        m_all = m_ref[...]
        m_sel = jnp.maximum(m_own, jnp.max(jnp.where(sel, m_all, -jnp.inf), axis=-1, keepdims=True))
        w = jnp.where(sel, jnp.exp(m_all - m_sel), 0.0)
        w_own = jnp.exp(m_own - m_sel)
        den = l_own * w_own + jnp.sum(w * l_ref[...], axis=-1, keepdims=True)
        num = o_own * w_own
        for j in range(n_past):
            num = num + w[:, j:j + 1] * acc_ref[j]
        out = num / den
        o_ref[0] = jnp.concatenate([out[h * t_new:(h + 1) * t_new] for h in range(n_heads)], axis=1)


def _sample_attention(q, k_new, v_new, cache_kt, cache_vt, page_table, layer):
    n, t_new, aw = q.shape
    _, _, n_heads, hd, page = cache_kt.shape
    n_pages = page_table.shape[1]
    npg = PAGES_PER_STEP
    assert n_pages % npg == 0 and MOBA_BLOCK % page == 0 and npg % (MOBA_BLOCK // page) == 0
    n_past = n_pages * page // MOBA_BLOCK
    assert n_past <= LANES
    rows = n_heads * t_new

    def page_spec(r):
        return pl.BlockSpec((1, 1, n_heads, hd, page),
                            lambda ni, s, pt: (layer, pt[ni, s * npg + r], 0, 0, 0))

    new_spec = pl.BlockSpec((1, t_new, aw), lambda ni, s, pt: (ni, 0, 0))
    grid_spec = pltpu.PrefetchScalarGridSpec(
        num_scalar_prefetch=1,
        grid=(n, n_pages // npg),
        in_specs=[new_spec, new_spec, new_spec] + [page_spec(r) for r in range(npg)] * 2,
        out_specs=new_spec,
        scratch_shapes=[pltpu.VMEM((rows, LANES), F32),
                        pltpu.VMEM((rows, LANES), F32),
                        pltpu.VMEM((aw, LANES), F32),
                        pltpu.VMEM((n_past, rows, hd), F32)],
    )
    return pl.pallas_call(
        functools.partial(_sample_attn_kernel, n_heads=n_heads, hd=hd, scale=hd ** -0.5),
        grid_spec=grid_spec,
        out_shape=jax.ShapeDtypeStruct(q.shape, F32),
        compiler_params=_cparams(("arbitrary", "arbitrary")),
        name="sample_attention",
    )(page_table, q, k_new, v_new, *([cache_kt] * npg), *([cache_vt] * npg))


def _split_bf16(x):
    hi = x.astype(BF16)
    return hi, (x - hi.astype(F32)).astype(BF16)


def _post_kernel(att_ref, pz_ref, pre_ref, gab_ref, x_ref, gt1_ref, sh2_ref, sc2_ref,
                 wao_ref, wpool_ref, pscale_ref, wout_ref, gpost1_ref, gpre2_ref, wrh_ref, wrl_ref, br_ref,
                 x1_ref, h2_ref, topi_ref, gates_ref, *, pos0, zero_first_prefix):
    bb, tt, d = x_ref.shape
    rows = bb * tt
    i = pl.program_id(1)
    n_groups = len(POOL_WINDOWS)
    gc = pz_ref.shape[2] // n_groups

    pz = pz_ref[...]
    prefix = pre_ref[...]
    if zero_first_prefix:
        prefix = jnp.where(i == 0, 0.0, prefix)
    xp = jnp.concatenate([prefix, pz], axis=1)
    pos = pos0 + i * tt + lax.broadcasted_iota(jnp.int32, (1, tt, 1), 1)
    pbs = []
    for g, w in enumerate(POOL_WINDOWS):
        a = xp[:, :, g * gc:(g + 1) * gc]
        span = 1
        while span < w:
            a = a + pltpu.roll(a, span, 1)
            span *= 2
        win = a[:, POOL_PREFIX_ROWS:, :]
        cnt = jnp.minimum(w, pos + 1).astype(F32)
        pooled = win / cnt - pz[:, :, g * gc:(g + 1) * gc]
        pbs.append(jnp.dot(pooled.reshape(rows, gc).astype(BF16), wpool_ref[g], preferred_element_type=F32))
    pb = jnp.concatenate(pbs, axis=1) * pscale_ref[...]

    att = jnp.dot(att_ref[...].reshape(rows, att_ref.shape[2]).astype(BF16), wao_ref[...],
                  preferred_element_type=F32)
    gab = gab_ref[...].reshape(rows, 2 * d).astype(F32)
    mix = _sigmoid(gab[:, :d]) * att + _sigmoid(gab[:, d:]) * pb
    mix = jnp.dot(mix.astype(BF16), wout_ref[...], preferred_element_type=F32)
    x1 = x_ref[...] + gt1_ref[...] * _rms(mix, gpost1_ref[...]).reshape(bb, tt, d)
    x1_ref[...] = x1
    h2 = (_rms(x1, gpre2_ref[...]) * (1.0 + sc2_ref[...]) + sh2_ref[...]).reshape(rows, d)
    _store_tile_rows(h2_ref, h2)

    h_hi, h_lo = _split_bf16(h2)
    logits = (jnp.dot(h_hi, wrh_ref[...], preferred_element_type=F32)
              + (jnp.dot(h_lo, wrh_ref[...], preferred_element_type=F32)
                 + jnp.dot(h_hi, wrl_ref[...], preferred_element_type=F32))) + br_ref[...]
    n_exp = logits.shape[1]
    lane = lax.broadcasted_iota(jnp.int32, logits.shape, 1).astype(F32)
    kcol = lax.broadcasted_iota(jnp.int32, (rows, MOE_TOP_K), 1)
    vals = jnp.zeros((rows, MOE_TOP_K), F32)
    idxs = jnp.zeros((rows, MOE_TOP_K), F32)
    v = logits
    for k in range(MOE_TOP_K):
        mx = jnp.max(v, axis=-1, keepdims=True)
        ix = jnp.min(jnp.where(v == mx, lane, float(n_exp)), axis=-1, keepdims=True)
        vals = jnp.where(kcol == k, mx, vals)
        idxs = jnp.where(kcol == k, ix, idxs)
        v = jnp.where(lane == ix, -jnp.inf, v)
    e = jnp.exp(vals - vals[:, 0:1])
    gates = e / jnp.sum(e, axis=-1, keepdims=True)
    topi_ref[...] = idxs.astype(jnp.int32).reshape(bb, tt, MOE_TOP_K)
    gates_ref[...] = gates.reshape(bb, tt, MOE_TOP_K)


def _post(att, pz, prefix, gab, x, mod3, weights, bb, tt, pos0, zero_first_prefix):
    bx, t, d = x.shape
    aw, pw = att.shape[2], pz.shape[2]
    grid = (bx // bb, t // tt)
    row = lambda width: pl.BlockSpec((bb, tt, width), lambda b, i: (b, i, 0))
    modspec = lambda slot: pl.BlockSpec((bb, 1, d), lambda b, i: (b, 0, slot))
    full = lambda a: pl.BlockSpec(a.shape, lambda b, i: (0,) * a.ndim)
    if zero_first_prefix:
        per_tile = tt // POOL_PREFIX_ROWS
        pre_spec = pl.BlockSpec((bb, POOL_PREFIX_ROWS, pw), lambda b, i: (b, jnp.maximum(i * per_tile - 1, 0), 0))
    else:
        assert t == tt
        pre_spec = pl.BlockSpec((bb, POOL_PREFIX_ROWS, pw), lambda b, i: (b, 0, 0))
    out = lambda width, dt: jax.ShapeDtypeStruct((bx, t, width), dt)
    return pl.pallas_call(
        functools.partial(_post_kernel, pos0=pos0, zero_first_prefix=zero_first_prefix),
        grid=grid,
        in_specs=[row(aw), row(pw), pre_spec, row(2 * d), row(d), modspec(2), modspec(3), modspec(4)]
                 + [full(w) for w in weights],
        out_specs=[row(d), pl.BlockSpec((bb * tt * SUBLANES, LANES), lambda b, i: (b * grid[1] + i, 0)),
                   row(MOE_TOP_K), row(MOE_TOP_K)],
        out_shape=[out(d, F32), jax.ShapeDtypeStruct((bx * t * SUBLANES, LANES), F32),
                   out(MOE_TOP_K, jnp.int32), out(MOE_TOP_K, F32)],
        compiler_params=_cparams(("arbitrary", "arbitrary")),
        name="post",
    )(att, pz, prefix, gab, x, mod3, mod3, mod3, *weights)


def _plan_kernel(topi_ref, rank_ref, counts_ref, base_ref):
    i = pl.program_id(0)
    tt = topi_ref.shape[0]
    n_exp = counts_ref.shape[1]

    @pl.when(i == 0)
    def _init():
        base_ref[...] = jnp.zeros(base_ref.shape, F32)

    topi = topi_ref[...]
    lane = lax.broadcasted_iota(jnp.int32, (tt, n_exp), 1)
    onehots = [jnp.where(topi[:, k:k + 1] == lane, 1.0, 0.0) for k in range(MOE_TOP_K)]
    tot = onehots[0]
    for oh in onehots[1:]:
        tot = tot + oh
    tri = jnp.where(lax.broadcasted_iota(jnp.int32, (tt, tt), 0) > lax.broadcasted_iota(jnp.int32, (tt, tt), 1),
                    1.0, 0.0).astype(BF16)
    before = jnp.dot(tri, tot.astype(BF16), preferred_element_type=F32) + base_ref[...]
    kcol = lax.broadcasted_iota(jnp.int32, (tt, MOE_TOP_K), 1)
    rank = jnp.zeros((tt, MOE_TOP_K), F32)
    for k in range(MOE_TOP_K):
        rank = jnp.where(kcol == k, jnp.sum(onehots[k] * before, axis=-1, keepdims=True), rank)
    rank_ref[...] = rank.astype(jnp.int32)
    base_ref[...] = base_ref[...] + jnp.sum(tot, axis=0, keepdims=True)
    counts_ref[...] = base_ref[...].astype(jnp.int32)


def _moe_plan(topi, n_exp):
    t = topi.shape[0]
    tt = PLAN_ROWS
    assert t % tt == 0
    return pl.pallas_call(
        _plan_kernel,
        grid=(t // tt,),
        in_specs=[pl.BlockSpec((tt, MOE_TOP_K), lambda i: (i, 0))],
        out_specs=[pl.BlockSpec((tt, MOE_TOP_K), lambda i: (i, 0)),
                   pl.BlockSpec((1, n_exp), lambda i: (0, 0))],
        out_shape=[jax.ShapeDtypeStruct((t, MOE_TOP_K), jnp.int32),
                   jax.ShapeDtypeStruct((1, n_exp), jnp.int32)],
        scratch_shapes=[pltpu.VMEM((1, n_exp), F32)],
        compiler_params=_cparams(("arbitrary",)),
        name="moe_plan",
    )(topi)


def _dispatch_kernel(last_blk_ref, dest_ref, h_ref, xs_ref, zero_ref, sem, zsem):
    i = pl.program_id(0)
    tile_rows = h_ref.shape[0]
    tt = tile_rows // SUBLANES
    n_fill = last_blk_ref.shape[0]

    @pl.when(i == 0)
    def _zero_padding():
        zero_ref[...] = jnp.zeros(zero_ref.shape, F32)

        def fill(e):
            start = pl.multiple_of(jnp.maximum(last_blk_ref[e], 0), SUBLANES)
            return pltpu.make_async_copy(zero_ref, xs_ref.at[pl.ds(start, zero_ref.shape[0]), :], zsem)

        for e in range(n_fill):
            @pl.when(last_blk_ref[e] >= 0)
            def _start(e=e):
                fill(e).start()
        for e in range(n_fill):
            @pl.when(last_blk_ref[e] >= 0)
            def _wait(e=e):
                fill(e).wait()

    def row_copy(r, k):
        dst = pl.multiple_of(dest_ref[0, 0, r * MOE_TOP_K + k], SUBLANES)
        return pltpu.make_async_copy(h_ref.at[pl.ds(pl.multiple_of(r * SUBLANES, SUBLANES), SUBLANES), :],
                                     xs_ref.at[pl.ds(dst, SUBLANES), :], sem)

    def issue(r, carry):
        for k in range(MOE_TOP_K):
            row_copy(r, k).start()
        return carry

    lax.fori_loop(0, tt, issue, 0, unroll=ISSUE_UNROLL)
    for k in range(MOE_TOP_K):
        pltpu.make_async_copy(h_ref, xs_ref.at[pl.ds(0, tile_rows), :], sem).wait()


def _moe_dispatch(last_blk, dest3, h3_all, p_rows, bm):
    tile_rows = DISPATCH_ROWS * SUBLANES
    assert h3_all.shape[0] % tile_rows == 0
    grid_spec = pltpu.PrefetchScalarGridSpec(
        num_scalar_prefetch=1,
        grid=(h3_all.shape[0] // tile_rows,),
        in_specs=[pl.BlockSpec((1, 1, DISPATCH_ROWS * MOE_TOP_K), lambda i, lb: (i, 0, 0), memory_space=pltpu.SMEM),
                  pl.BlockSpec((tile_rows, LANES), lambda i, lb: (i, 0))],
        out_specs=pl.BlockSpec(memory_space=pl.ANY),
        scratch_shapes=[pltpu.VMEM((bm * SUBLANES, LANES), F32),
                        pltpu.SemaphoreType.DMA(()), pltpu.SemaphoreType.DMA(())],
    )
    return pl.pallas_call(
        _dispatch_kernel,
        grid_spec=grid_spec,
        out_shape=jax.ShapeDtypeStruct((p_rows * SUBLANES, LANES), F32),
        compiler_params=_cparams(("arbitrary",)),
        name="moe_dispatch",
    )(last_blk, dest3, h3_all)


def _expert_kernel(blk_e_ref, blk_first_ref, n_used_ref, xs_ref, wgu_ref, bgu_ref, wd_ref, bd_ref, ys_ref,
                   wgu_bf_ref, wd_bf_ref):
    del blk_e_ref
    i = pl.program_id(0)
    dff = wd_ref.shape[1]
    bm = xs_ref.shape[0] // SUBLANES
    used = i < n_used_ref[0]

    @pl.when(used & (blk_first_ref[i] == 1))
    def _cast_weights():
        wgu_bf_ref[...] = wgu_ref[0].astype(BF16)
        wd_bf_ref[...] = wd_ref[0].astype(BF16)

    @pl.when(used)
    def _compute():
        x = _load_tile_rows(xs_ref, bm).astype(BF16)
        gu = jnp.dot(x, wgu_bf_ref[...], preferred_element_type=F32) + bgu_ref[0]
        gt = jnp.minimum(gu[:, :dff], SWIGLU_LIMIT)
        up = jnp.clip(gu[:, dff:], -SWIGLU_LIMIT, SWIGLU_LIMIT)
        act = (up + 1.0) * (gt * _sigmoid(gt * SWIGLU_ALPHA))
        _store_tile_rows(ys_ref, jnp.dot(act.astype(BF16), wd_bf_ref[...], preferred_element_type=F32) + bd_ref[0])

    @pl.when(jnp.logical_not(used))
    def _skip():
        ys_ref[...] = jnp.zeros(ys_ref.shape, F32)


def _moe_experts(blk_e, blk_first, n_used, xs3, w_gu, b_gu3, w_down, b_down3):
    bm = EXPERT_ROWS
    n_exp, d, two_f = w_gu.shape
    dff = w_down.shape[1]
    assert d == SUBLANES * LANES
    tile_rows = bm * SUBLANES
    grid_spec = pltpu.PrefetchScalarGridSpec(
        num_scalar_prefetch=3,
        grid=(xs3.shape[0] // tile_rows,),
        in_specs=[pl.BlockSpec((tile_rows, LANES), lambda i, be, bf, nu: (i, 0)),
                  pl.BlockSpec((1, d, two_f), lambda i, be, bf, nu: (be[i], 0, 0)),
                  pl.BlockSpec((1, 1, two_f), lambda i, be, bf, nu: (be[i], 0, 0)),
                  pl.BlockSpec((1, dff, d), lambda i, be, bf, nu: (be[i], 0, 0)),
                  pl.BlockSpec((1, 1, d), lambda i, be, bf, nu: (be[i], 0, 0))],
        out_specs=pl.BlockSpec((tile_rows, LANES), lambda i, be, bf, nu: (i, 0)),
        scratch_shapes=[pltpu.VMEM((d, two_f), BF16), pltpu.VMEM((dff, d), BF16)],
    )
    return pl.pallas_call(
        _expert_kernel,
        grid_spec=grid_spec,
        out_shape=jax.ShapeDtypeStruct(xs3.shape, F32),
        compiler_params=_cparams(("arbitrary",)),
        name="moe_experts",
    )(blk_e, blk_first, n_used, xs3, w_gu, b_gu3, w_down, b_down3)


def _combine_kernel(dest_ref, gates_ref, x1_ref, gt2_ref, gpost2_ref, ys_ref, y_ref, buf_ref, sem):
    bb, tt, d = x1_ref.shape
    rows = bb * tt

    def row_copy(r, k):
        src = pl.multiple_of(dest_ref[0, 0, r * MOE_TOP_K + k], SUBLANES)
        return pltpu.make_async_copy(ys_ref.at[pl.ds(src, SUBLANES), :],
                                     buf_ref.at[k, pl.ds(pl.multiple_of(r * SUBLANES, SUBLANES), SUBLANES), :], sem)

    def issue(r, carry):
        for k in range(MOE_TOP_K):
            row_copy(r, k).start()
        return carry

    lax.fori_loop(0, rows, issue, 0, unroll=ISSUE_UNROLL)
    for k in range(MOE_TOP_K):
        pltpu.make_async_copy(ys_ref.at[pl.ds(0, rows * SUBLANES), :], buf_ref.at[k], sem).wait()

    gates = gates_ref[...].reshape(rows, MOE_TOP_K)
    ff = gates[:, 0:1] * _load_tile_rows(buf_ref, rows, lead=(0,))
    for k in range(1, MOE_TOP_K):
        ff = ff + gates[:, k:k + 1] * _load_tile_rows(buf_ref, rows, lead=(k,))
    y_ref[...] = x1_ref[...] + gt2_ref[...] * _rms(ff, gpost2_ref[...]).reshape(bb, tt, d)


def _moe_combine(dest3, gates, x1, mod3, g_post2, ys, bb, tt):
    bx, t, d = x1.shape
    rows = bb * tt
    n_t = t // tt
    row = lambda width: pl.BlockSpec((bb, tt, width), lambda b, i: (b, i, 0))
    return pl.pallas_call(
        _combine_kernel,
        grid=(bx // bb, n_t),
        in_specs=[pl.BlockSpec((1, 1, rows * MOE_TOP_K), lambda b, i: (b * n_t + i, 0, 0), memory_space=pltpu.SMEM),
                  row(MOE_TOP_K), row(d),
                  pl.BlockSpec((bb, 1, d), lambda b, i: (b, 0, 5)),
                  pl.BlockSpec((1, d), lambda b, i: (0, 0)),
                  pl.BlockSpec(memory_space=pl.ANY)],
        out_specs=row(d),
        out_shape=jax.ShapeDtypeStruct((bx, t, d), F32),
        scratch_shapes=[pltpu.VMEM((MOE_TOP_K, rows * SUBLANES, LANES), F32), pltpu.SemaphoreType.DMA(())],
        compiler_params=_cparams(("arbitrary", "arbitrary")),
        name="moe_combine",
    )(dest3, gates, x1, mod3, g_post2, ys)


def _layer(xp, xs, cp, cs, cache_kt, cache_vt, state_pool_l, page_table, layer, p):
    b, s_len, d = xp.shape
    n, t_new, _ = xs.shape
    n_heads, hd, page = cache_kt.shape[2:]
    aw = n_heads * hd
    pw = state_pool_l.shape[2]
    n_exp = p['w_router'].shape[1]
    sb = min(SAMPLE_SEQS, n)
    pr = min(PROMPT_ROWS, s_len)

    mod = _mod(jnp.concatenate([cp, cs], axis=0), p['w_mod'], p['b_mod'])
    mod_p = mod[:b].reshape(b, 1, N_MOD * d)
    mod_s = mod[b:].reshape(n, 1, N_MOD * d)

    w_in_bf = p['w_in'].astype(BF16)
    w_kvt_bf = p['w_in'][:, aw:3 * aw].T.astype(BF16)
    qp, kp, ktp, vtp, vtbp, pzp, gabp = _in_proj_prompt(xp, mod_p, p['g_pre1'], w_in_bf, w_kvt_bf, aw, pw, pr,
                                                        hd ** -0.5 * LOG2_E)
    qs, ks, vs, pzs, gabs = _in_proj_sample(xs, mod_s, p['g_pre1'], w_in_bf, aw, pw, sb)

    att_p = _prompt_attention(qp, kp, vtbp, n_heads, hd)
    att_s = _sample_attention(qs, ks, vs, cache_kt, cache_vt, page_table, layer)

    wr_hi = p['w_router'].astype(BF16)
    wr_lo = (p['w_router'] - wr_hi.astype(F32)).astype(BF16)
    post_w = (p['w_attn_o'].astype(BF16), p['w_pool'].astype(BF16), p['pool_scale'], p['w_out'].astype(BF16),
              p['g_post1'], p['g_pre2'], wr_hi, wr_lo, p['b_router'])
    x1p, h2p, tip, gp = _post(att_p, pzp, pzp, gabp, xp, mod_p, post_w, 1, pr, 0, True)
    state16 = jnp.concatenate([jnp.zeros((n, 1, pw), F32), state_pool_l], axis=1)
    x1s, h2s, tis, gs = _post(att_s, pzs, state16, gabs, xs, mod_s, post_w, sb, t_new,
                              page_table.shape[1] * page, False)

    tp, ts = b * s_len, n * t_new
    h3_all = jnp.concatenate([h2p, h2s], axis=0)
    topi = jnp.concatenate([tip.reshape(tp, MOE_TOP_K), tis.reshape(ts, MOE_TOP_K)], axis=0)
    t_all = tp + ts
    rank, counts = _moe_plan(topi, n_exp)
    bm = EXPERT_ROWS
    counts = counts[0]
    padded = (counts + bm - 1) // bm * bm
    pends = jnp.cumsum(padded)
    pstarts = pends - padded
    n_blocks = (t_all * MOE_TOP_K + n_exp * (bm - 1) + bm - 1) // bm
    blk_start = jnp.arange(n_blocks, dtype=jnp.int32) * bm
    blk_e = jnp.minimum(jnp.sum((pends[None, :] <= blk_start[:, None]).astype(jnp.int32), axis=1), n_exp - 1)
    blk_first = jnp.concatenate([jnp.ones((1,), jnp.int32), (blk_e[1:] != blk_e[:-1]).astype(jnp.int32)])
    n_used = (pends[-1:] // bm).astype(jnp.int32)
    tail_blk = n_used + jnp.arange(n_exp, dtype=jnp.int32)
    last_blk = jnp.concatenate([jnp.where(padded > 0, (pends - bm) * SUBLANES, -1),
                                jnp.where(tail_blk < n_blocks, tail_blk * (bm * SUBLANES), -1)]).astype(jnp.int32)
    onehot_e = topi[:, :, None] == jnp.arange(n_exp, dtype=jnp.int32)
    dest = (jnp.sum(jnp.where(onehot_e, pstarts.astype(jnp.int32), 0), axis=-1) + rank) * SUBLANES

    xs_sorted = _moe_dispatch(last_blk, dest.reshape(t_all // DISPATCH_ROWS, 1, DISPATCH_ROWS * MOE_TOP_K),
                              h3_all, n_blocks * bm, bm)
    ys = _moe_experts(blk_e, blk_first, n_used, xs_sorted, p['w_gu'], p['b_gu'][:, None, :],
                      p['w_down'], p['b_down'][:, None, :])

    cr = min(COMBINE_ROWS, s_len)
    csb = min(max(COMBINE_ROWS // t_new, 1), n)
    dest_p = dest[:tp].reshape(tp // cr, 1, cr * MOE_TOP_K)
    dest_s = dest[tp:].reshape(ts // (csb * t_new), 1, csb * t_new * MOE_TOP_K)
    yp = _moe_combine(dest_p, gp, x1p, mod_p, p['g_post2'], ys, 1, cr)
    ysm = _moe_combine(dest_s, gs, x1s, mod_s, p['g_post2'], ys, csb, t_new)

    k_out = ktp.reshape(b, n_heads, hd, s_len).transpose(0, 3, 1, 2)
    v_out = vtp.reshape(b, n_heads, hd, s_len).transpose(0, 3, 1, 2)
    pool_p = pzp[:, s_len - (POOL_PREFIX_ROWS - 1):, :]
    pool_s = jnp.concatenate([state_pool_l, pzs], axis=1)[:, -(POOL_PREFIX_ROWS - 1):, :]
    return (yp, ysm, k_out, v_out, pool_p,
            ks.reshape(n, t_new, n_heads, hd), vs.reshape(n, t_new, n_heads, hd), pool_s)


def kernel(x_prompt, x_sample, c_prompt, c_sample, cache_k, cache_v, state_pool, page_table, w_mod, b_mod, g_pre1, g_post1, g_pre2, g_post2, w_in, w_attn_o, w_pool, pool_scale, w_out, w_router, b_router, w_gu, b_gu, w_down, b_down):
    depth = w_mod.shape[0]
    hp, hs = x_prompt, x_sample
    cache_kt = cache_k.transpose(0, 1, 3, 4, 2)
    cache_vt = cache_v.transpose(0, 1, 3, 4, 2)
    outs = [[] for _ in range(6)]
    for l in range(depth):
        p = {'w_mod': w_mod[l], 'b_mod': b_mod[l][None, :], 'g_pre1': g_pre1[l][None, :],
             'g_post1': g_post1[l][None, :], 'g_pre2': g_pre2[l][None, :], 'g_post2': g_post2[l][None, :],
             'w_in': w_in[l], 'w_attn_o': w_attn_o[l], 'w_pool': w_pool[l], 'pool_scale': pool_scale[l][None, :],
             'w_out': w_out[l], 'w_router': w_router[l], 'b_router': b_router[l][None, :],
             'w_gu': w_gu[l], 'b_gu': b_gu[l], 'w_down': w_down[l], 'b_down': b_down[l]}
        hp, hs, kp, vp, pp, ks, vs, ps = _layer(hp, hs, c_prompt, c_sample, cache_kt, cache_vt, state_pool[l],
                                                page_table, l, p)
        for lst, val in zip(outs, (kp, vp, pp, ks, vs, ps)):
            lst.append(val)
    return (hp, hs) + tuple(jnp.stack(o) for o in outs)
```

```python
import functools

import jax
import jax.numpy as jnp
from jax import lax
from jax.experimental import pallas as pl
from jax.experimental.pallas import tpu as pltpu

F32 = jnp.float32
BF16 = jnp.bfloat16
HIGHEST = lax.Precision.HIGHEST

MOBA_BLOCK = 256
MOBA_TOPK = 3
POOL_WINDOWS = (2, 4, 8, 16)
POOL_PREFIX_ROWS = 16
MOE_TOP_K = 4
SWIGLU_LIMIT = 7.0
SWIGLU_ALPHA = 1.702
N_MOD = 6
NORM_EPS = 1e-6
MASK_BIAS = -1e30
LOG2_E = 1.4426950408889634

LANES = 128
SUBLANES = 8
VMEM_LIMIT = 56 * 1024 * 1024
PROMPT_ROWS = 512
SAMPLE_SEQS = 16
PAGES_PER_STEP = 8
PLAN_ROWS = 512
DISPATCH_ROWS = 512
EXPERT_ROWS = 512
COMBINE_ROWS = 256
ISSUE_UNROLL = 4
DMA_QUEUES = 2


def _cparams(sem):
    return pltpu.CompilerParams(dimension_semantics=sem, vmem_limit_bytes=VMEM_LIMIT)


def _rms(x, g):
    return x * lax.rsqrt(jnp.mean(x * x, axis=-1, keepdims=True) + NORM_EPS) * g


def _dot_nt(a, b, precision=None):
    return lax.dot_general(a, b, (((1,), (1,)), ((), ())), preferred_element_type=F32, precision=precision)


def _sigmoid(x):
    return 0.5 * jnp.tanh(0.5 * x) + 0.5


def _load_tile_rows(ref, rows, lead=()):
    return jnp.concatenate([ref[lead + (pl.ds(c, rows, stride=SUBLANES), slice(None))] for c in range(SUBLANES)],
                           axis=1)


def _store_tile_rows(ref, x):
    rows = x.shape[0]
    for c in range(SUBLANES):
        ref[pl.ds(c, rows, stride=SUBLANES), :] = x[:, c * LANES:(c + 1) * LANES]


def _mod_kernel(c_ref, w_ref, b_ref, o_ref):
    c = c_ref[...]
    s = c * jax.nn.sigmoid(c)
    o_ref[...] = jnp.dot(s, w_ref[...], preferred_element_type=F32, precision=HIGHEST) + b_ref[...]


def _mod(c_all, w_mod, b_mod):
    n, d = c_all.shape
    return pl.pallas_call(
        _mod_kernel,
        grid=(N_MOD,),
        in_specs=[pl.BlockSpec((n, d), lambda j: (0, 0)),
                  pl.BlockSpec((d, d), lambda j: (0, j)),
                  pl.BlockSpec((1, d), lambda j: (0, j))],
        out_specs=pl.BlockSpec((n, d), lambda j: (0, j)),
        out_shape=jax.ShapeDtypeStruct((n, N_MOD * d), F32),
        compiler_params=_cparams(("arbitrary",)),
        name="mod",
    )(c_all, w_mod, b_mod)


def _modulated(x_ref, sh_ref, sc_ref, g_ref):
    bb, tt, d = x_ref.shape
    h = _rms(x_ref[...], g_ref[...]) * (1.0 + sc_ref[...]) + sh_ref[...]
    return h.reshape(bb * tt, d).astype(BF16)


def _in_proj_sample_kernel(x_ref, sh_ref, sc_ref, g_ref, w_ref, q_ref, k_ref, v_ref, pz_ref, gab_ref, *, aw, pw):
    bb, tt, d = x_ref.shape
    hb = _modulated(x_ref, sh_ref, sc_ref, g_ref)
    col = 0
    for ref, width in ((q_ref, aw), (k_ref, aw), (v_ref, aw), (pz_ref, pw), (gab_ref, 2 * d)):
        u = jnp.dot(hb, w_ref[:, col:col + width], preferred_element_type=F32)
        ref[...] = u.reshape(bb, tt, width)
        col += width


def _in_proj_prompt_kernel(x_ref, sh_ref, sc_ref, g_ref, w_ref, wkvt_ref,
                           q_ref, k_ref, kt_ref, vt_ref, vtb_ref, pz_ref, gab_ref, *, aw, pw, qscale):
    _, tt, d = x_ref.shape
    hb = _modulated(x_ref, sh_ref, sc_ref, g_ref)
    q_ref[0] = (jnp.dot(hb, w_ref[:, 0:aw], preferred_element_type=F32) * qscale).astype(BF16)
    k_ref[0] = jnp.dot(hb, w_ref[:, aw:2 * aw], preferred_element_type=F32).astype(BF16)
    col = 3 * aw
    pz_ref[0] = jnp.dot(hb, w_ref[:, col:col + pw], preferred_element_type=F32)
    gab_ref[0] = jnp.dot(hb, w_ref[:, col + pw:col + pw + 2 * d], preferred_element_type=F32).astype(BF16)
    kvt = _dot_nt(wkvt_ref[...], hb)
    kt_ref[0] = kvt[0:aw]
    vt_ref[0] = kvt[aw:2 * aw]
    for j in range(tt // MOBA_BLOCK):
        vtb_ref[0, j] = kvt[aw:2 * aw, j * MOBA_BLOCK:(j + 1) * MOBA_BLOCK].astype(BF16)


def _in_proj_sample(x, mod3, g_pre1, w_in_bf, aw, pw, bb):
    bx, t, d = x.shape
    row = lambda width: pl.BlockSpec((bb, t, width), lambda b: (b, 0, 0))
    modspec = lambda slot: pl.BlockSpec((bb, 1, d), lambda b: (b, 0, slot))
    out = lambda width: jax.ShapeDtypeStruct((bx, t, width), F32)
    return pl.pallas_call(
        functools.partial(_in_proj_sample_kernel, aw=aw, pw=pw),
        grid=(bx // bb,),
        in_specs=[row(d), modspec(0), modspec(1),
                  pl.BlockSpec((1, d), lambda b: (0, 0)),
                  pl.BlockSpec(w_in_bf.shape, lambda b: (0, 0))],
        out_specs=[row(aw), row(aw), row(aw), row(pw), row(2 * d)],
        out_shape=[out(aw), out(aw), out(aw), out(pw), out(2 * d)],
        compiler_params=_cparams(("arbitrary",)),
        name="in_proj_sample",
    )(x, mod3, mod3, g_pre1, w_in_bf)


def _in_proj_prompt(x, mod3, g_pre1, w_in_bf, w_kvt_bf, aw, pw, tt, qscale):
    bx, t, d = x.shape
    assert tt % MOBA_BLOCK == 0 and t % tt == 0
    nb, per = t // MOBA_BLOCK, tt // MOBA_BLOCK
    row = lambda width: pl.BlockSpec((1, tt, width), lambda b, i: (b, i, 0))
    col = pl.BlockSpec((1, aw, tt), lambda b, i: (b, 0, i))
    modspec = lambda slot: pl.BlockSpec((1, 1, d), lambda b, i: (b, 0, slot))
    return pl.pallas_call(
        functools.partial(_in_proj_prompt_kernel, aw=aw, pw=pw, qscale=qscale),
        grid=(bx, t // tt),
        in_specs=[row(d), modspec(0), modspec(1),
                  pl.BlockSpec((1, d), lambda b, i: (0, 0)),
                  pl.BlockSpec(w_in_bf.shape, lambda b, i: (0, 0)),
                  pl.BlockSpec(w_kvt_bf.shape, lambda b, i: (0, 0))],
        out_specs=[row(aw), row(aw), col, col,
                   pl.BlockSpec((1, per, aw, MOBA_BLOCK), lambda b, i: (b, i, 0, 0)),
                   row(pw), row(2 * d)],
        out_shape=[jax.ShapeDtypeStruct((bx, t, aw), BF16), jax.ShapeDtypeStruct((bx, t, aw), BF16),
                   jax.ShapeDtypeStruct((bx, aw, t), F32), jax.ShapeDtypeStruct((bx, aw, t), F32),
                   jax.ShapeDtypeStruct((bx, nb, aw, MOBA_BLOCK), BF16),
                   jax.ShapeDtypeStruct((bx, t, pw), F32), jax.ShapeDtypeStruct((bx, t, 2 * d), BF16)],
        compiler_params=_cparams(("arbitrary", "arbitrary")),
        name="in_proj_prompt",
    )(x, mod3, mod3, g_pre1, w_in_bf, w_kvt_bf)


def _moba_select(gate, n_past, n_cand, axis):
    slot = lax.broadcasted_iota(jnp.int32, gate.shape, axis)
    past = slot < n_past
    g = jnp.where(past, gate, -jnp.inf)
    rank = jnp.zeros(gate.shape, jnp.int32)
    for j in range(n_cand):
        other = g[:, j:j + 1] if axis == 1 else g[j:j + 1, :]
        beats = jnp.where(other > g, 1, jnp.where(other == g, jnp.where(j < slot, 1, 0), 0))
        rank = rank + jnp.where(j < n_past, beats, 0)
    return past & (rank < MOBA_TOPK)


def _prompt_attn_kernel(q_ref, k_ref, vtb_ref, o_ref, kaug_ref, kmean_ref, qaug_ref, *, n_heads, hd):
    i = pl.program_id(1)
    s_len = k_ref.shape[1]
    tq = q_ref.shape[1]
    nb = s_len // MOBA_BLOCK
    nbp = kmean_ref.shape[1]
    slots = kaug_ref.shape[2] - hd

    @pl.when(i == 0)
    def _prepare():
        kf = k_ref[0].astype(F32)
        key_blk = lax.broadcasted_iota(jnp.int32, (s_len, slots), 0) // MOBA_BLOCK
        onehot = jnp.where(key_blk == lax.broadcasted_iota(jnp.int32, (s_len, slots), 1), 1.0, 0.0).astype(BF16)
        kmean = kf.reshape(nb, MOBA_BLOCK, kf.shape[1]).sum(axis=1) * (1.0 / MOBA_BLOCK)
        kmean_ref[...] = jnp.zeros(kmean_ref.shape, F32)
        for h in range(n_heads):
            kaug_ref[h, :, 0:hd] = k_ref[0, :, h * hd:(h + 1) * hd]
            kaug_ref[h, :, hd:hd + slots] = onehot
            kmean_ref[h, 0:nb, :] = kmean[:, h * hd:(h + 1) * hd]

    qt = q_ref[0].astype(F32).T
    slot = lax.broadcasted_iota(jnp.int32, (nbp, tq), 0)
    gates = [jnp.dot(kmean_ref[h], qt[h * hd:(h + 1) * hd], preferred_element_type=F32, precision=HIGHEST)
             for h in range(n_heads)]
    for h in range(n_heads):
        sel = _moba_select(gates[h], i, nb, axis=0) | (slot == i)
        bias = jnp.where(sel, 0.0, MASK_BIAS)
        parts = [qt[h * hd:(h + 1) * hd], bias]
        if slots > nbp:
            parts.append(jnp.zeros((slots - nbp, tq), F32))
        qaug_ref[h] = jnp.concatenate(parts, axis=0).astype(BF16)

    def scores(h, r0):
        return jnp.dot(kaug_ref[h, pl.ds(r0, MOBA_BLOCK), :], qaug_ref[h], preferred_element_type=F32)

    key_id = lax.broadcasted_iota(jnp.int32, (MOBA_BLOCK, tq), 0)
    qry_id = lax.broadcasted_iota(jnp.int32, (MOBA_BLOCK, tq), 1)
    diag0 = pl.multiple_of(i * MOBA_BLOCK, MOBA_BLOCK)
    ss = [scores(h, diag0) for h in range(n_heads)]
    soft = []
    for h in range(n_heads):
        s = jnp.where(key_id <= qry_id, ss[h], -jnp.inf)
        m = jnp.max(s, axis=0, keepdims=True)
        p = jnp.exp2(s - m)
        soft.append((m, jnp.sum(p, axis=0, keepdims=True), p.astype(BF16)))
    state = [(m, l, jnp.dot(vtb_ref[0, i, h * hd:(h + 1) * hd, :], p, preferred_element_type=F32))
             for h, (m, l, p) in enumerate(soft)]

    def body(j, state):
        r0 = pl.multiple_of(j * MOBA_BLOCK, MOBA_BLOCK)
        ss = [scores(h, r0) for h in range(n_heads)]
        soft = []
        for h in range(n_heads):
            m_old, l_old, _ = state[h]
            m_new = jnp.maximum(m_old, jnp.max(ss[h], axis=0, keepdims=True))
            alpha = jnp.exp2(m_old - m_new)
            p = jnp.exp2(ss[h] - m_new)
            soft.append((m_new, alpha, alpha * l_old + jnp.sum(p, axis=0, keepdims=True), p.astype(BF16)))
        new_state = []
        for h in range(n_heads):
            m_new, alpha, l_new, p = soft[h]
            acc = alpha * state[h][2] + jnp.dot(vtb_ref[0, j, h * hd:(h + 1) * hd, :], p,
                                                preferred_element_type=F32)
            new_state.append((m_new, l_new, acc))
        return tuple(new_state)

    state = lax.fori_loop(0, i, body, tuple(state))
    o_ref[0] = jnp.concatenate([acc / l for _, l, acc in state], axis=0).T.astype(BF16)


def _prompt_attention(q_bf, k_bf, vtb, n_heads, hd):
    b, s_len, aw = q_bf.shape
    assert s_len % MOBA_BLOCK == 0
    nb = s_len // MOBA_BLOCK
    nbp = -(-nb // SUBLANES) * SUBLANES
    slots = LANES - hd
    assert nbp <= slots
    tq = MOBA_BLOCK
    return pl.pallas_call(
        functools.partial(_prompt_attn_kernel, n_heads=n_heads, hd=hd),
        grid=(b, s_len // tq),
        in_specs=[pl.BlockSpec((1, tq, aw), lambda bi, i: (bi, i, 0)),
                  pl.BlockSpec((1, s_len, aw), lambda bi, i: (bi, 0, 0)),
                  pl.BlockSpec((1, nb, aw, MOBA_BLOCK), lambda bi, i: (bi, 0, 0, 0))],
        out_specs=pl.BlockSpec((1, tq, aw), lambda bi, i: (bi, i, 0)),
        out_shape=jax.ShapeDtypeStruct((b, s_len, aw), BF16),
        scratch_shapes=[pltpu.VMEM((n_heads, s_len, hd + slots), BF16),
                        pltpu.VMEM((n_heads, nbp, hd), F32),
                        pltpu.VMEM((n_heads, hd + slots, tq), BF16)],
        compiler_params=_cparams(("arbitrary", "arbitrary")),
        name="prompt_attention",
    )(q_bf, k_bf, vtb)


def _head_diag(o, n_heads, hd, rows_per_head):
    row_head = lax.broadcasted_iota(jnp.int32, (o.shape[0], hd), 0) // rows_per_head
    out = jnp.zeros((o.shape[0], hd), F32)
    for h in range(n_heads):
        out = out + jnp.where(row_head == h, o[:, h * hd:(h + 1) * hd], 0.0)
    return out


def _sample_attn_kernel(pt_ref, q_ref, kn_ref, vn_ref, *rest, n_heads, hd, scale):
    del pt_ref
    npg = PAGES_PER_STEP
    k_pages = rest[:npg]
    v_pages = rest[npg:2 * npg]
    o_ref = rest[2 * npg]
    m_ref, l_ref, ksum_ref, acc_ref = rest[2 * npg + 1:]
    step = pl.program_id(1)
    n_steps = pl.num_programs(1)
    t_new = q_ref.shape[1]
    aw = n_heads * hd
    rows = n_heads * t_new
    page = k_pages[0].shape[4]
    pages_per_blk = MOBA_BLOCK // page
    n_past = acc_ref.shape[0]

    @pl.when(step == 0)
    def _init():
        m_ref[...] = jnp.zeros(m_ref.shape, F32)
        l_ref[...] = jnp.zeros(l_ref.shape, F32)
        ksum_ref[...] = jnp.zeros(ksum_ref.shape, F32)

    q_rep = jnp.concatenate([q_ref[0]] * n_heads, axis=0)
    own_head = (lax.broadcasted_iota(jnp.int32, (rows, aw), 0) // t_new
                == lax.broadcasted_iota(jnp.int32, (rows, aw), 1) // hd)
    q_bd = jnp.where(own_head, q_rep, 0.0)
    q_bd_bf = (q_bd * scale).astype(BF16)
    blk_lane = lax.broadcasted_iota(jnp.int32, (1, LANES), 1)
    ones_bf = jnp.ones((MOBA_BLOCK, LANES), BF16)

    n_blk = npg // pages_per_blk
    scored = []
    for bi in range(n_blk):
        pr = range(bi * pages_per_blk, (bi + 1) * pages_per_blk)
        kt_bf = jnp.concatenate([k_pages[r][0, 0].reshape(aw, page) for r in pr], axis=1).astype(BF16)
        scored.append((jnp.dot(q_bd_bf, kt_bf, preferred_element_type=F32),
                       jnp.dot(kt_bf, ones_bf, preferred_element_type=F32)))
    soft = []
    for s, _ in scored:
        m = jnp.max(s, axis=-1, keepdims=True)
        p = jnp.exp(s - m)
        soft.append((m, jnp.sum(p, axis=-1, keepdims=True), p.astype(BF16)))
    for bi in range(n_blk):
        pr = range(bi * pages_per_blk, (bi + 1) * pages_per_blk)
        vt_bf = jnp.concatenate([v_pages[r][0, 0].reshape(aw, page) for r in pr], axis=1).astype(BF16)
        m, l, p = soft[bi]
        o = _dot_nt(p, vt_bf)
        j = step * n_blk + bi
        here = blk_lane == j
        m_ref[...] = jnp.where(here, m, m_ref[...])
        l_ref[...] = jnp.where(here, l, l_ref[...])
        ksum_ref[...] = jnp.where(here, scored[bi][1], ksum_ref[...])
        acc_ref[j] = _head_diag(o, n_heads, hd, t_new)

    @pl.when(step == n_steps - 1)
    def _finish():
        s = _dot_nt(q_bd * scale, kn_ref[0])
        causal = (lax.broadcasted_iota(jnp.int32, (rows, t_new), 1)
                  <= lax.broadcasted_iota(jnp.int32, (rows, t_new), 0) % t_new)
        s = jnp.where(causal, s, -jnp.inf)
        m_own = jnp.max(s, axis=-1, keepdims=True)
        p = jnp.exp(s - m_own)
        l_own = jnp.sum(p, axis=-1, keepdims=True)
        o_own = _head_diag(jnp.dot(p, vn_ref[0], preferred_element_type=F32), n_heads, hd, t_new)

        gate = jnp.dot(q_bd, ksum_ref[...] * (1.0 / MOBA_BLOCK), preferred_element_type=F32, precision=HIGHEST)
        sel = _moba_select(gate, n_past, n_past, axis=1)
        m_all = m_ref[...]
        m_sel = jnp.maximum(m_own, jnp.max(jnp.where(sel, m_all, -jnp.inf), axis=-1, keepdims=True))
        w = jnp.where(sel, jnp.exp(m_all - m_sel), 0.0)
        w_own = jnp.exp(m_own - m_sel)
        den = l_own * w_own + jnp.sum(w * l_ref[...], axis=-1, keepdims=True)
        num = o_own * w_own
        for j in range(n_past):
            num = num + w[:, j:j + 1] * acc_ref[j]
        out = num / den
        o_ref[0] = jnp.concatenate([out[h * t_new:(h + 1) * t_new] for h in range(n_heads)], axis=1)


def _sample_attention(q, k_new, v_new, cache_kt, cache_vt, page_table, layer):
    n, t_new, aw = q.shape
    _, _, n_heads, hd, page = cache_kt.shape
    n_pages = page_table.shape[1]
    npg = PAGES_PER_STEP
    assert n_pages % npg == 0 and MOBA_BLOCK % page == 0 and npg % (MOBA_BLOCK // page) == 0
    n_past = n_pages * page // MOBA_BLOCK
    assert n_past <= LANES
    rows = n_heads * t_new

    def page_spec(r):
        return pl.BlockSpec((1, 1, n_heads, hd, page),
                            lambda ni, s, pt: (layer, pt[ni, s * npg + r], 0, 0, 0))

    new_spec = pl.BlockSpec((1, t_new, aw), lambda ni, s, pt: (ni, 0, 0))
    grid_spec = pltpu.PrefetchScalarGridSpec(
        num_scalar_prefetch=1,
        grid=(n, n_pages // npg),
        in_specs=[new_spec, new_spec, new_spec] + [page_spec(r) for r in range(npg)] * 2,
        out_specs=new_spec,
        scratch_shapes=[pltpu.VMEM((rows, LANES), F32),
                        pltpu.VMEM((rows, LANES), F32),
                        pltpu.VMEM((aw, LANES), F32),
                        pltpu.VMEM((n_past, rows, hd), F32)],
    )
    return pl.pallas_call(
        functools.partial(_sample_attn_kernel, n_heads=n_heads, hd=hd, scale=hd ** -0.5),
        grid_spec=grid_spec,
        out_shape=jax.ShapeDtypeStruct(q.shape, F32),
        compiler_params=_cparams(("arbitrary", "arbitrary")),
        name="sample_attention",
    )(page_table, q, k_new, v_new, *([cache_kt] * npg), *([cache_vt] * npg))


def _split_bf16(x):
    hi = x.astype(BF16)
    return hi, (x - hi.astype(F32)).astype(BF16)


def _post_kernel(att_ref, pz_ref, pre_ref, gab_ref, x_ref, gt1_ref, sh2_ref, sc2_ref,
                 wao_ref, wpool_ref, pscale_ref, wout_ref, gpost1_ref, gpre2_ref, wrh_ref, wrl_ref, br_ref,
                 x1_ref, h2_ref, topi_ref, gates_ref, *, pos0, zero_first_prefix):
    bb, tt, d = x_ref.shape
    rows = bb * tt
    i = pl.program_id(1)
    n_groups = len(POOL_WINDOWS)
    gc = pz_ref.shape[2] // n_groups

    pz = pz_ref[...]
    prefix = pre_ref[...]
    if zero_first_prefix:
        prefix = jnp.where(i == 0, 0.0, prefix)
    xp = jnp.concatenate([prefix, pz], axis=1)
    pos = pos0 + i * tt + lax.broadcasted_iota(jnp.int32, (1, tt, 1), 1)
    pbs = []
    for g, w in enumerate(POOL_WINDOWS):
        a = xp[:, :, g * gc:(g + 1) * gc]
        span = 1
        while span < w:
            a = a + pltpu.roll(a, span, 1)
            span *= 2
        win = a[:, POOL_PREFIX_ROWS:, :]
        cnt = jnp.minimum(w, pos + 1).astype(F32)
        pooled = win / cnt - pz[:, :, g * gc:(g + 1) * gc]
        pbs.append(jnp.dot(pooled.reshape(rows, gc).astype(BF16), wpool_ref[g], preferred_element_type=F32))
    pb = jnp.concatenate(pbs, axis=1) * pscale_ref[...]

    att = jnp.dot(att_ref[...].reshape(rows, att_ref.shape[2]).astype(BF16), wao_ref[...],
                  preferred_element_type=F32)
    gab = gab_ref[...].reshape(rows, 2 * d).astype(F32)
    mix = _sigmoid(gab[:, :d]) * att + _sigmoid(gab[:, d:]) * pb
    mix = jnp.dot(mix.astype(BF16), wout_ref[...], preferred_element_type=F32)
    x1 = x_ref[...] + gt1_ref[...] * _rms(mix, gpost1_ref[...]).reshape(bb, tt, d)
    x1_ref[...] = x1
    h2 = (_rms(x1, gpre2_ref[...]) * (1.0 + sc2_ref[...]) + sh2_ref[...]).reshape(rows, d)
    _store_tile_rows(h2_ref, h2)

    h_hi, h_lo = _split_bf16(h2)
    logits = (jnp.dot(h_hi, wrh_ref[...], preferred_element_type=F32)
              + (jnp.dot(h_lo, wrh_ref[...], preferred_element_type=F32)
                 + jnp.dot(h_hi, wrl_ref[...], preferred_element_type=F32))) + br_ref[...]
    n_exp = logits.shape[1]
    lane = lax.broadcasted_iota(jnp.int32, logits.shape, 1).astype(F32)
    kcol = lax.broadcasted_iota(jnp.int32, (rows, MOE_TOP_K), 1)
    vals = jnp.zeros((rows, MOE_TOP_K), F32)
    idxs = jnp.zeros((rows, MOE_TOP_K), F32)
    v = logits
    for k in range(MOE_TOP_K):
        mx = jnp.max(v, axis=-1, keepdims=True)
        ix = jnp.min(jnp.where(v == mx, lane, float(n_exp)), axis=-1, keepdims=True)
        vals = jnp.where(kcol == k, mx, vals)
        idxs = jnp.where(kcol == k, ix, idxs)
        v = jnp.where(lane == ix, -jnp.inf, v)
    e = jnp.exp(vals - vals[:, 0:1])
    gates = e / jnp.sum(e, axis=-1, keepdims=True)
    topi_ref[...] = idxs.astype(jnp.int32).reshape(bb, tt, MOE_TOP_K)
    gates_ref[...] = gates.reshape(bb, tt, MOE_TOP_K)


def _post(att, pz, prefix, gab, x, mod3, weights, bb, tt, pos0, zero_first_prefix):
    bx, t, d = x.shape
    aw, pw = att.shape[2], pz.shape[2]
    grid = (bx // bb, t // tt)
    row = lambda width: pl.BlockSpec((bb, tt, width), lambda b, i: (b, i, 0))
    modspec = lambda slot: pl.BlockSpec((bb, 1, d), lambda b, i: (b, 0, slot))
    full = lambda a: pl.BlockSpec(a.shape, lambda b, i: (0,) * a.ndim)
    if zero_first_prefix:
        per_tile = tt // POOL_PREFIX_ROWS
        pre_spec = pl.BlockSpec((bb, POOL_PREFIX_ROWS, pw), lambda b, i: (b, jnp.maximum(i * per_tile - 1, 0), 0))
    else:
        assert t == tt
        pre_spec = pl.BlockSpec((bb, POOL_PREFIX_ROWS, pw), lambda b, i: (b, 0, 0))
    out = lambda width, dt: jax.ShapeDtypeStruct((bx, t, width), dt)
    return pl.pallas_call(
        functools.partial(_post_kernel, pos0=pos0, zero_first_prefix=zero_first_prefix),
        grid=grid,
        in_specs=[row(aw), row(pw), pre_spec, row(2 * d), row(d), modspec(2), modspec(3), modspec(4)]
                 + [full(w) for w in weights],
        out_specs=[row(d), pl.BlockSpec((bb * tt * SUBLANES, LANES), lambda b, i: (b * grid[1] + i, 0)),
                   row(MOE_TOP_K), row(MOE_TOP_K)],
        out_shape=[out(d, F32), jax.ShapeDtypeStruct((bx * t * SUBLANES, LANES), F32),
                   out(MOE_TOP_K, jnp.int32), out(MOE_TOP_K, F32)],
        compiler_params=_cparams(("arbitrary", "arbitrary")),
        name="post",
    )(att, pz, prefix, gab, x, mod3, mod3, mod3, *weights)


def _plan_kernel(topi_ref, rank_ref, counts_ref, base_ref):
    i = pl.program_id(0)
    tt = topi_ref.shape[0]
    n_exp = counts_ref.shape[1]

    @pl.when(i == 0)
    def _init():
        base_ref[...] = jnp.zeros(base_ref.shape, F32)

    topi = topi_ref[...]
    lane = lax.broadcasted_iota(jnp.int32, (tt, n_exp), 1)
    onehots = [jnp.where(topi[:, k:k + 1] == lane, 1.0, 0.0) for k in range(MOE_TOP_K)]
    tot = onehots[0]
    for oh in onehots[1:]:
        tot = tot + oh
    tri = jnp.where(lax.broadcasted_iota(jnp.int32, (tt, tt), 0) > lax.broadcasted_iota(jnp.int32, (tt, tt), 1),
                    1.0, 0.0).astype(BF16)
    before = jnp.dot(tri, tot.astype(BF16), preferred_element_type=F32) + base_ref[...]
    kcol = lax.broadcasted_iota(jnp.int32, (tt, MOE_TOP_K), 1)
    rank = jnp.zeros((tt, MOE_TOP_K), F32)
    for k in range(MOE_TOP_K):
        rank = jnp.where(kcol == k, jnp.sum(onehots[k] * before, axis=-1, keepdims=True), rank)
    rank_ref[...] = rank.astype(jnp.int32)
    base_ref[...] = base_ref[...] + jnp.sum(tot, axis=0, keepdims=True)
    counts_ref[...] = base_ref[...].astype(jnp.int32)


def _moe_plan(topi, n_exp):
    t = topi.shape[0]
    tt = PLAN_ROWS
    assert t % tt == 0
    return pl.pallas_call(
        _plan_kernel,
        grid=(t // tt,),
        in_specs=[pl.BlockSpec((tt, MOE_TOP_K), lambda i: (i, 0))],
        out_specs=[pl.BlockSpec((tt, MOE_TOP_K), lambda i: (i, 0)),
                   pl.BlockSpec((1, n_exp), lambda i: (0, 0))],
        out_shape=[jax.ShapeDtypeStruct((t, MOE_TOP_K), jnp.int32),
                   jax.ShapeDtypeStruct((1, n_exp), jnp.int32)],
        scratch_shapes=[pltpu.VMEM((1, n_exp), F32)],
        compiler_params=_cparams(("arbitrary",)),
        name="moe_plan",
    )(topi)


def _dispatch_kernel(last_blk_ref, dest_ref, ha_ref, hb_ref, xs_ref, zero_ref, sem, zsem, *, steps_a):
    i = pl.program_id(0)
    tile_rows = ha_ref.shape[0]
    tt = tile_rows // SUBLANES
    n_fill = last_blk_ref.shape[0]

    @pl.when(i == 0)
    def _zero_padding():
        zero_ref[...] = jnp.zeros(zero_ref.shape, F32)

        def fill(e):
            start = pl.multiple_of(jnp.maximum(last_blk_ref[e], 0), SUBLANES)
            return pltpu.make_async_copy(zero_ref, xs_ref.at[pl.ds(start, zero_ref.shape[0]), :], zsem)

        for e in range(n_fill):
            @pl.when(last_blk_ref[e] >= 0)
            def _start(e=e):
                fill(e).start()
        for e in range(n_fill):
            @pl.when(last_blk_ref[e] >= 0)
            def _wait(e=e):
                fill(e).wait()

    def scatter_rows(h_ref):
        def row_copy(r, k):
            dst = pl.multiple_of(dest_ref[0, 0, r * MOE_TOP_K + k], SUBLANES)
            return pltpu.make_async_copy(h_ref.at[pl.ds(pl.multiple_of(r * SUBLANES, SUBLANES), SUBLANES), :],
                                         xs_ref.at[pl.ds(dst, SUBLANES), :], sem)

        def issue(r, carry):
            for k in range(MOE_TOP_K):
                row_copy(r, k).start(priority=k % DMA_QUEUES)
            return carry

        lax.fori_loop(0, tt, issue, 0, unroll=ISSUE_UNROLL)
        for k in range(MOE_TOP_K):
            pltpu.make_async_copy(h_ref, xs_ref.at[pl.ds(0, tile_rows), :], sem).wait()

    @pl.when(i < steps_a)
    def _group_a():
        scatter_rows(ha_ref)

    @pl.when(i >= steps_a)
    def _group_b():
        scatter_rows(hb_ref)


def _moe_dispatch(last_blk, dest3, h3_a, h3_b, p_rows, bm):
    tile_rows = DISPATCH_ROWS * SUBLANES
    assert h3_a.shape[0] % tile_rows == 0 and h3_b.shape[0] % tile_rows == 0
    steps_a, steps_b = h3_a.shape[0] // tile_rows, h3_b.shape[0] // tile_rows
    grid_spec = pltpu.PrefetchScalarGridSpec(
        num_scalar_prefetch=1,
        grid=(steps_a + steps_b,),
        in_specs=[pl.BlockSpec((1, 1, DISPATCH_ROWS * MOE_TOP_K), lambda i, lb: (i, 0, 0), memory_space=pltpu.SMEM),
                  pl.BlockSpec((tile_rows, LANES), lambda i, lb: (jnp.minimum(i, steps_a - 1), 0)),
                  pl.BlockSpec((tile_rows, LANES), lambda i, lb: (jnp.maximum(i - steps_a, 0), 0))],
        out_specs=pl.BlockSpec(memory_space=pl.ANY),
        scratch_shapes=[pltpu.VMEM((bm * SUBLANES, LANES), F32),
                        pltpu.SemaphoreType.DMA(()), pltpu.SemaphoreType.DMA(())],
    )
    return pl.pallas_call(
        functools.partial(_dispatch_kernel, steps_a=steps_a),
        grid_spec=grid_spec,
        out_shape=jax.ShapeDtypeStruct((p_rows * SUBLANES, LANES), F32),
        compiler_params=_cparams(("arbitrary",)),
        name="moe_dispatch",
    )(last_blk, dest3, h3_a, h3_b)


def _expert_kernel(blk_e_ref, blk_first_ref, n_used_ref, xs_ref, wgu_ref, bgu_ref, wd_ref, bd_ref, ys_ref,
                   wgu_bf_ref, wd_bf_ref):
    del blk_e_ref
    i = pl.program_id(0)
    dff = wd_ref.shape[1]
    bm = xs_ref.shape[0] // SUBLANES
    used = i < n_used_ref[0]

    @pl.when(used & (blk_first_ref[i] == 1))
    def _cast_weights():
        wgu_bf_ref[...] = wgu_ref[0].astype(BF16)
        wd_bf_ref[...] = wd_ref[0].astype(BF16)

    @pl.when(used)
    def _compute():
        x = _load_tile_rows(xs_ref, bm).astype(BF16)
        gu = jnp.dot(x, wgu_bf_ref[...], preferred_element_type=F32) + bgu_ref[0]
        gt = jnp.minimum(gu[:, :dff], SWIGLU_LIMIT)
        up = jnp.clip(gu[:, dff:], -SWIGLU_LIMIT, SWIGLU_LIMIT)
        act = (up + 1.0) * (gt * _sigmoid(gt * SWIGLU_ALPHA))
        _store_tile_rows(ys_ref, jnp.dot(act.astype(BF16), wd_bf_ref[...], preferred_element_type=F32) + bd_ref[0])

    @pl.when(jnp.logical_not(used))
    def _skip():
        ys_ref[...] = jnp.zeros(ys_ref.shape, F32)


def _moe_experts(blk_e, blk_first, n_used, xs3, w_gu, b_gu3, w_down, b_down3):
    bm = EXPERT_ROWS
    n_exp, d, two_f = w_gu.shape
    dff = w_down.shape[1]
    assert d == SUBLANES * LANES
    tile_rows = bm * SUBLANES
    grid_spec = pltpu.PrefetchScalarGridSpec(
        num_scalar_prefetch=3,
        grid=(xs3.shape[0] // tile_rows,),
        in_specs=[pl.BlockSpec((tile_rows, LANES), lambda i, be, bf, nu: (i, 0)),
                  pl.BlockSpec((1, d, two_f), lambda i, be, bf, nu: (be[i], 0, 0)),
                  pl.BlockSpec((1, 1, two_f), lambda i, be, bf, nu: (be[i], 0, 0)),
                  pl.BlockSpec((1, dff, d), lambda i, be, bf, nu: (be[i], 0, 0)),
                  pl.BlockSpec((1, 1, d), lambda i, be, bf, nu: (be[i], 0, 0))],
        out_specs=pl.BlockSpec((tile_rows, LANES), lambda i, be, bf, nu: (i, 0)),
        scratch_shapes=[pltpu.VMEM((d, two_f), BF16), pltpu.VMEM((dff, d), BF16)],
    )
    return pl.pallas_call(
        _expert_kernel,
        grid_spec=grid_spec,
        out_shape=jax.ShapeDtypeStruct(xs3.shape, F32),
        compiler_params=_cparams(("arbitrary",)),
        name="moe_experts",
    )(blk_e, blk_first, n_used, xs3, w_gu, b_gu3, w_down, b_down3)


def _combine_kernel(dest_ref, gates_ref, x1_ref, gt2_ref, gpost2_ref, ys_ref, y_ref, buf_ref, sem):
    bb, tt, d = x1_ref.shape
    rows = bb * tt

    def row_copy(r, k):
        src = pl.multiple_of(dest_ref[0, 0, r * MOE_TOP_K + k], SUBLANES)
        return pltpu.make_async_copy(ys_ref.at[pl.ds(src, SUBLANES), :],
                                     buf_ref.at[k, pl.ds(pl.multiple_of(r * SUBLANES, SUBLANES), SUBLANES), :], sem)

    def issue(r, carry):
        for k in range(MOE_TOP_K):
            row_copy(r, k).start(priority=k % DMA_QUEUES)
        return carry

    lax.fori_loop(0, rows, issue, 0, unroll=ISSUE_UNROLL)
    for k in range(MOE_TOP_K):
        pltpu.make_async_copy(ys_ref.at[pl.ds(0, rows * SUBLANES), :], buf_ref.at[k], sem).wait()

    gates = gates_ref[...].reshape(rows, MOE_TOP_K)
    ff = gates[:, 0:1] * _load_tile_rows(buf_ref, rows, lead=(0,))
    for k in range(1, MOE_TOP_K):
        ff = ff + gates[:, k:k + 1] * _load_tile_rows(buf_ref, rows, lead=(k,))
    y_ref[...] = x1_ref[...] + gt2_ref[...] * _rms(ff, gpost2_ref[...]).reshape(bb, tt, d)


def _moe_combine(dest3, gates, x1, mod3, g_post2, ys, bb, tt):
    bx, t, d = x1.shape
    rows = bb * tt
    n_t = t // tt
    row = lambda width: pl.BlockSpec((bb, tt, width), lambda b, i: (b, i, 0))
    return pl.pallas_call(
        _combine_kernel,
        grid=(bx // bb, n_t),
        in_specs=[pl.BlockSpec((1, 1, rows * MOE_TOP_K), lambda b, i: (b * n_t + i, 0, 0), memory_space=pltpu.SMEM),
                  row(MOE_TOP_K), row(d),
                  pl.BlockSpec((bb, 1, d), lambda b, i: (b, 0, 5)),
                  pl.BlockSpec((1, d), lambda b, i: (0, 0)),
                  pl.BlockSpec(memory_space=pl.ANY)],
        out_specs=row(d),
        out_shape=jax.ShapeDtypeStruct((bx, t, d), F32),
        scratch_shapes=[pltpu.VMEM((MOE_TOP_K, rows * SUBLANES, LANES), F32), pltpu.SemaphoreType.DMA(())],
        compiler_params=_cparams(("arbitrary", "arbitrary")),
        name="moe_combine",
    )(dest3, gates, x1, mod3, g_post2, ys)


def _layer(xp, xs, cp, cs, cache_kt, cache_vt, state_pool_l, page_table, layer, p):
    b, s_len, d = xp.shape
    n, t_new, _ = xs.shape
    n_heads, hd, page = cache_kt.shape[2:]
    aw = n_heads * hd
    pw = state_pool_l.shape[2]
    n_exp = p['w_router'].shape[1]
    sb = min(SAMPLE_SEQS, n)
    pr = min(PROMPT_ROWS, s_len)

    mod = _mod(jnp.concatenate([cp, cs], axis=0), p['w_mod'], p['b_mod'])
    mod_p = mod[:b].reshape(b, 1, N_MOD * d)
    mod_s = mod[b:].reshape(n, 1, N_MOD * d)

    w_in_bf = p['w_in'].astype(BF16)
    w_kvt_bf = p['w_in'][:, aw:3 * aw].T.astype(BF16)
    qp, kp, ktp, vtp, vtbp, pzp, gabp = _in_proj_prompt(xp, mod_p, p['g_pre1'], w_in_bf, w_kvt_bf, aw, pw, pr,
                                                        hd ** -0.5 * LOG2_E)
    qs, ks, vs, pzs, gabs = _in_proj_sample(xs, mod_s, p['g_pre1'], w_in_bf, aw, pw, sb)

    att_p = _prompt_attention(qp, kp, vtbp, n_heads, hd)
    att_s = _sample_attention(qs, ks, vs, cache_kt, cache_vt, page_table, layer)

    wr_hi = p['w_router'].astype(BF16)
    wr_lo = (p['w_router'] - wr_hi.astype(F32)).astype(BF16)
    post_w = (p['w_attn_o'].astype(BF16), p['w_pool'].astype(BF16), p['pool_scale'], p['w_out'].astype(BF16),
              p['g_post1'], p['g_pre2'], wr_hi, wr_lo, p['b_router'])
    x1p, h2p, tip, gp = _post(att_p, pzp, pzp, gabp, xp, mod_p, post_w, 1, pr, 0, True)
    state16 = jnp.concatenate([jnp.zeros((n, 1, pw), F32), state_pool_l], axis=1)
    x1s, h2s, tis, gs = _post(att_s, pzs, state16, gabs, xs, mod_s, post_w, sb, t_new,
                              page_table.shape[1] * page, False)

    tp, ts = b * s_len, n * t_new
    topi = jnp.concatenate([tip.reshape(tp, MOE_TOP_K), tis.reshape(ts, MOE_TOP_K)], axis=0)
    t_all = tp + ts
    rank, counts = _moe_plan(topi, n_exp)
    bm = EXPERT_ROWS
    counts = counts[0]
    padded = (counts + bm - 1) // bm * bm
    pends = jnp.cumsum(padded)
    pstarts = pends - padded
    n_blocks = (t_all * MOE_TOP_K + n_exp * (bm - 1) + bm - 1) // bm
    blk_start = jnp.arange(n_blocks, dtype=jnp.int32) * bm
    blk_e = jnp.minimum(jnp.sum((pends[None, :] <= blk_start[:, None]).astype(jnp.int32), axis=1), n_exp - 1)
    blk_first = jnp.concatenate([jnp.ones((1,), jnp.int32), (blk_e[1:] != blk_e[:-1]).astype(jnp.int32)])
    n_used = (pends[-1:] // bm).astype(jnp.int32)
    tail_blk = n_used + jnp.arange(n_exp, dtype=jnp.int32)
    last_blk = jnp.concatenate([jnp.where(padded > 0, (pends - bm) * SUBLANES, -1),
                                jnp.where(tail_blk < n_blocks, tail_blk * (bm * SUBLANES), -1)]).astype(jnp.int32)
    onehot_e = topi[:, :, None] == jnp.arange(n_exp, dtype=jnp.int32)
    dest = (jnp.sum(jnp.where(onehot_e, pstarts.astype(jnp.int32), 0), axis=-1) + rank) * SUBLANES

    xs_sorted = _moe_dispatch(last_blk, dest.reshape(t_all // DISPATCH_ROWS, 1, DISPATCH_ROWS * MOE_TOP_K),
                              h2p, h2s, n_blocks * bm, bm)
    ys = _moe_experts(blk_e, blk_first, n_used, xs_sorted, p['w_gu'], p['b_gu'][:, None, :],
                      p['w_down'], p['b_down'][:, None, :])

    cr = min(COMBINE_ROWS, s_len)
    csb = min(max(COMBINE_ROWS // t_new, 1), n)
    dest_p = dest[:tp].reshape(tp // cr, 1, cr * MOE_TOP_K)
    dest_s = dest[tp:].reshape(ts // (csb * t_new), 1, csb * t_new * MOE_TOP_K)
    yp = _moe_combine(dest_p, gp, x1p, mod_p, p['g_post2'], ys, 1, cr)
    ysm = _moe_combine(dest_s, gs, x1s, mod_s, p['g_post2'], ys, csb, t_new)

    k_out = ktp.reshape(b, n_heads, hd, s_len).transpose(0, 3, 1, 2)
    v_out = vtp.reshape(b, n_heads, hd, s_len).transpose(0, 3, 1, 2)
    pool_p = pzp[:, s_len - (POOL_PREFIX_ROWS - 1):, :]
    pool_s = jnp.concatenate([state_pool_l, pzs], axis=1)[:, -(POOL_PREFIX_ROWS - 1):, :]
    return (yp, ysm, k_out, v_out, pool_p,
            ks.reshape(n, t_new, n_heads, hd), vs.reshape(n, t_new, n_heads, hd), pool_s)


def kernel(x_prompt, x_sample, c_prompt, c_sample, cache_k, cache_v, state_pool, page_table, w_mod, b_mod, g_pre1, g_post1, g_pre2, g_post2, w_in, w_attn_o, w_pool, pool_scale, w_out, w_router, b_router, w_gu, b_gu, w_down, b_down):
    depth = w_mod.shape[0]
    hp, hs = x_prompt, x_sample
    cache_kt = cache_k.transpose(0, 1, 3, 4, 2)
    cache_vt = cache_v.transpose(0, 1, 3, 4, 2)
    outs = [[] for _ in range(6)]
    for l in range(depth):
        p = {'w_mod': w_mod[l], 'b_mod': b_mod[l][None, :], 'g_pre1': g_pre1[l][None, :],
             'g_post1': g_post1[l][None, :], 'g_pre2': g_pre2[l][None, :], 'g_post2': g_post2[l][None, :],
             'w_in': w_in[l], 'w_attn_o': w_attn_o[l], 'w_pool': w_pool[l], 'pool_scale': pool_scale[l][None, :],
             'w_out': w_out[l], 'w_router': w_router[l], 'b_router': b_router[l][None, :],
             'w_gu': w_gu[l], 'b_gu': b_gu[l], 'w_down': w_down[l], 'b_down': b_down[l]}
        hp, hs, kp, vp, pp, ks, vs, ps = _layer(hp, hs, c_prompt, c_sample, cache_kt, cache_vt, state_pool[l],
                                                page_table, l, p)
        for lst, val in zip(outs, (kp, vp, pp, ks, vs, ps)):
            lst.append(val)
    return (hp, hs) + tuple(jnp.stack(o) for o in outs)
```

```python
import functools

import jax
import jax.numpy as jnp
from jax import lax
from jax.experimental import pallas as pl
from jax.experimental.pallas import tpu as pltpu

F32 = jnp.float32
BF16 = jnp.bfloat16
HIGHEST = lax.Precision.HIGHEST

MOBA_BLOCK = 256
MOBA_TOPK = 3
POOL_WINDOWS = (2, 4, 8, 16)
POOL_PREFIX_ROWS = 16
MOE_TOP_K = 4
SWIGLU_LIMIT = 7.0
SWIGLU_ALPHA = 1.702
N_MOD = 6
NORM_EPS = 1e-6
MASK_BIAS = -1e30
LOG2_E = 1.4426950408889634

LANES = 128
SUBLANES = 8
VMEM_LIMIT = 56 * 1024 * 1024
PROMPT_ROWS = 512
SAMPLE_SEQS = 16
PAGES_PER_STEP = 8
PLAN_ROWS = 512
DISPATCH_ROWS = 512
EXPERT_ROWS = 512
COMBINE_ROWS = 256
ISSUE_UNROLL = 8
DMA_QUEUES = 2


def _cparams(sem):
    return pltpu.CompilerParams(dimension_semantics=sem, vmem_limit_bytes=VMEM_LIMIT)


def _rms(x, g):
    return x * lax.rsqrt(jnp.mean(x * x, axis=-1, keepdims=True) + NORM_EPS) * g


def _dot_nt(a, b, precision=None):
    return lax.dot_general(a, b, (((1,), (1,)), ((), ())), preferred_element_type=F32, precision=precision)


def _sigmoid(x):
    return 0.5 * jnp.tanh(0.5 * x) + 0.5


def _load_tile_rows(ref, rows, lead=()):
    return jnp.concatenate([ref[lead + (pl.ds(c, rows, stride=SUBLANES), slice(None))] for c in range(SUBLANES)],
                           axis=1)


def _store_tile_rows(ref, x):
    rows = x.shape[0]
    for c in range(SUBLANES):
        ref[pl.ds(c, rows, stride=SUBLANES), :] = x[:, c * LANES:(c + 1) * LANES]


def _mod_kernel(c_ref, w_ref, b_ref, o_ref):
    c = c_ref[...]
    s = c * jax.nn.sigmoid(c)
    o_ref[...] = jnp.dot(s, w_ref[...], preferred_element_type=F32, precision=HIGHEST) + b_ref[...]


def _mod(c_all, w_mod, b_mod):
    n, d = c_all.shape
    return pl.pallas_call(
        _mod_kernel,
        grid=(N_MOD,),
        in_specs=[pl.BlockSpec((n, d), lambda j: (0, 0)),
                  pl.BlockSpec((d, d), lambda j: (0, j)),
                  pl.BlockSpec((1, d), lambda j: (0, j))],
        out_specs=pl.BlockSpec((n, d), lambda j: (0, j)),
        out_shape=jax.ShapeDtypeStruct((n, N_MOD * d), F32),
        compiler_params=_cparams(("arbitrary",)),
        name="mod",
    )(c_all, w_mod, b_mod)


def _modulated(x_ref, sh_ref, sc_ref, g_ref):
    bb, tt, d = x_ref.shape
    h = _rms(x_ref[...], g_ref[...]) * (1.0 + sc_ref[...]) + sh_ref[...]
    return h.reshape(bb * tt, d).astype(BF16)


def _in_proj_sample_kernel(x_ref, sh_ref, sc_ref, g_ref, w_ref, q_ref, k_ref, v_ref, pz_ref, gab_ref, *, aw, pw):
    bb, tt, d = x_ref.shape
    hb = _modulated(x_ref, sh_ref, sc_ref, g_ref)
    col = 0
    for ref, width in ((q_ref, aw), (k_ref, aw), (v_ref, aw), (pz_ref, pw), (gab_ref, 2 * d)):
        u = jnp.dot(hb, w_ref[:, col:col + width], preferred_element_type=F32)
        ref[...] = u.reshape(bb, tt, width)
        col += width


def _in_proj_prompt_kernel(x_ref, sh_ref, sc_ref, g_ref, w_ref, wkvt_ref,
                           q_ref, k_ref, kt_ref, vt_ref, vtb_ref, pz_ref, gab_ref, *, aw, pw, qscale):
    _, tt, d = x_ref.shape
    hb = _modulated(x_ref, sh_ref, sc_ref, g_ref)
    q_ref[0] = (jnp.dot(hb, w_ref[:, 0:aw], preferred_element_type=F32) * qscale).astype(BF16)
    k_ref[0] = jnp.dot(hb, w_ref[:, aw:2 * aw], preferred_element_type=F32).astype(BF16)
    col = 3 * aw
    pz_ref[0] = jnp.dot(hb, w_ref[:, col:col + pw], preferred_element_type=F32)
    gab_ref[0] = jnp.dot(hb, w_ref[:, col + pw:col + pw + 2 * d], preferred_element_type=F32).astype(BF16)
    kvt = _dot_nt(wkvt_ref[...], hb)
    kt_ref[0] = kvt[0:aw]
    vt_ref[0] = kvt[aw:2 * aw]
    for j in range(tt // MOBA_BLOCK):
        vtb_ref[0, j] = kvt[aw:2 * aw, j * MOBA_BLOCK:(j + 1) * MOBA_BLOCK].astype(BF16)


def _in_proj_sample(x, mod3, g_pre1, w_in_bf, aw, pw, bb):
    bx, t, d = x.shape
    row = lambda width: pl.BlockSpec((bb, t, width), lambda b: (b, 0, 0))
    modspec = lambda slot: pl.BlockSpec((bb, 1, d), lambda b: (b, 0, slot))
    out = lambda width: jax.ShapeDtypeStruct((bx, t, width), F32)
    return pl.pallas_call(
        functools.partial(_in_proj_sample_kernel, aw=aw, pw=pw),
        grid=(bx // bb,),
        in_specs=[row(d), modspec(0), modspec(1),
                  pl.BlockSpec((1, d), lambda b: (0, 0)),
                  pl.BlockSpec(w_in_bf.shape, lambda b: (0, 0))],
        out_specs=[row(aw), row(aw), row(aw), row(pw), row(2 * d)],
        out_shape=[out(aw), out(aw), out(aw), out(pw), out(2 * d)],
        compiler_params=_cparams(("arbitrary",)),
        name="in_proj_sample",
    )(x, mod3, mod3, g_pre1, w_in_bf)


def _in_proj_prompt(x, mod3, g_pre1, w_in_bf, w_kvt_bf, aw, pw, tt, qscale):
    bx, t, d = x.shape
    assert tt % MOBA_BLOCK == 0 and t % tt == 0
    nb, per = t // MOBA_BLOCK, tt // MOBA_BLOCK
    row = lambda width: pl.BlockSpec((1, tt, width), lambda b, i: (b, i, 0))
    col = pl.BlockSpec((1, aw, tt), lambda b, i: (b, 0, i))
    modspec = lambda slot: pl.BlockSpec((1, 1, d), lambda b, i: (b, 0, slot))
    return pl.pallas_call(
        functools.partial(_in_proj_prompt_kernel, aw=aw, pw=pw, qscale=qscale),
        grid=(bx, t // tt),
        in_specs=[row(d), modspec(0), modspec(1),
                  pl.BlockSpec((1, d), lambda b, i: (0, 0)),
                  pl.BlockSpec(w_in_bf.shape, lambda b, i: (0, 0)),
                  pl.BlockSpec(w_kvt_bf.shape, lambda b, i: (0, 0))],
        out_specs=[row(aw), row(aw), col, col,
                   pl.BlockSpec((1, per, aw, MOBA_BLOCK), lambda b, i: (b, i, 0, 0)),
                   row(pw), row(2 * d)],
        out_shape=[jax.ShapeDtypeStruct((bx, t, aw), BF16), jax.ShapeDtypeStruct((bx, t, aw), BF16),
                   jax.ShapeDtypeStruct((bx, aw, t), F32), jax.ShapeDtypeStruct((bx, aw, t), F32),
                   jax.ShapeDtypeStruct((bx, nb, aw, MOBA_BLOCK), BF16),
                   jax.ShapeDtypeStruct((bx, t, pw), F32), jax.ShapeDtypeStruct((bx, t, 2 * d), BF16)],
        compiler_params=_cparams(("arbitrary", "arbitrary")),
        name="in_proj_prompt",
    )(x, mod3, mod3, g_pre1, w_in_bf, w_kvt_bf)


def _moba_select(gate, n_past, n_cand, axis):
    slot = lax.broadcasted_iota(jnp.int32, gate.shape, axis)
    past = slot < n_past
    g = jnp.where(past, gate, -jnp.inf)
    rank = jnp.zeros(gate.shape, jnp.int32)
    for j in range(n_cand):
        other = g[:, j:j + 1] if axis == 1 else g[j:j + 1, :]
        beats = jnp.where(other > g, 1, jnp.where(other == g, jnp.where(j < slot, 1, 0), 0))
        rank = rank + jnp.where(j < n_past, beats, 0)
    return past & (rank < MOBA_TOPK)


def _prompt_attn_kernel(q_ref, k_ref, vtb_ref, o_ref, kaug_ref, kmean_ref, qaug_ref, *, n_heads, hd):
    i = pl.program_id(1)
    s_len = k_ref.shape[1]
    tq = q_ref.shape[1]
    nb = s_len // MOBA_BLOCK
    nbp = kmean_ref.shape[1]
    slots = kaug_ref.shape[2] - hd

    @pl.when(i == 0)
    def _prepare():
        kf = k_ref[0].astype(F32)
        key_blk = lax.broadcasted_iota(jnp.int32, (s_len, slots), 0) // MOBA_BLOCK
        onehot = jnp.where(key_blk == lax.broadcasted_iota(jnp.int32, (s_len, slots), 1), 1.0, 0.0).astype(BF16)
        kmean = kf.reshape(nb, MOBA_BLOCK, kf.shape[1]).sum(axis=1) * (1.0 / MOBA_BLOCK)
        kmean_ref[...] = jnp.zeros(kmean_ref.shape, F32)
        for h in range(n_heads):
            kaug_ref[h, :, 0:hd] = k_ref[0, :, h * hd:(h + 1) * hd]
            kaug_ref[h, :, hd:hd + slots] = onehot
            kmean_ref[h, 0:nb, :] = kmean[:, h * hd:(h + 1) * hd]

    qt = q_ref[0].astype(F32).T
    slot = lax.broadcasted_iota(jnp.int32, (nbp, tq), 0)
    gates = [jnp.dot(kmean_ref[h], qt[h * hd:(h + 1) * hd], preferred_element_type=F32, precision=HIGHEST)
             for h in range(n_heads)]
    for h in range(n_heads):
        sel = _moba_select(gates[h], i, nb, axis=0) | (slot == i)
        bias = jnp.where(sel, 0.0, MASK_BIAS)
        parts = [qt[h * hd:(h + 1) * hd], bias]
        if slots > nbp:
            parts.append(jnp.zeros((slots - nbp, tq), F32))
        qaug_ref[h] = jnp.concatenate(parts, axis=0).astype(BF16)

    def scores(h, r0):
        return jnp.dot(kaug_ref[h, pl.ds(r0, MOBA_BLOCK), :], qaug_ref[h], preferred_element_type=F32)

    key_id = lax.broadcasted_iota(jnp.int32, (MOBA_BLOCK, tq), 0)
    qry_id = lax.broadcasted_iota(jnp.int32, (MOBA_BLOCK, tq), 1)
    diag0 = pl.multiple_of(i * MOBA_BLOCK, MOBA_BLOCK)
    ss = [scores(h, diag0) for h in range(n_heads)]
    soft = []
    for h in range(n_heads):
        s = jnp.where(key_id <= qry_id, ss[h], -jnp.inf)
        m = jnp.max(s, axis=0, keepdims=True)
        p = jnp.exp2(s - m)
        soft.append((m, jnp.sum(p, axis=0, keepdims=True), p.astype(BF16)))
    state = [(m, l, jnp.dot(vtb_ref[0, i, h * hd:(h + 1) * hd, :], p, preferred_element_type=F32))
             for h, (m, l, p) in enumerate(soft)]

    def body(j, state):
        r0 = pl.multiple_of(j * MOBA_BLOCK, MOBA_BLOCK)
        ss = [scores(h, r0) for h in range(n_heads)]
        soft = []
        for h in range(n_heads):
            m_old, l_old, _ = state[h]
            m_new = jnp.maximum(m_old, jnp.max(ss[h], axis=0, keepdims=True))
            alpha = jnp.exp2(m_old - m_new)
            p = jnp.exp2(ss[h] - m_new)
            soft.append((m_new, alpha, alpha * l_old + jnp.sum(p, axis=0, keepdims=True), p.astype(BF16)))
        new_state = []
        for h in range(n_heads):
            m_new, alpha, l_new, p = soft[h]
            acc = alpha * state[h][2] + jnp.dot(vtb_ref[0, j, h * hd:(h + 1) * hd, :], p,
                                                preferred_element_type=F32)
            new_state.append((m_new, l_new, acc))
        return tuple(new_state)

    state = lax.fori_loop(0, i, body, tuple(state))
    o_ref[0] = jnp.concatenate([acc / l for _, l, acc in state], axis=0).T.astype(BF16)


def _prompt_attention(q_bf, k_bf, vtb, n_heads, hd):
    b, s_len, aw = q_bf.shape
    assert s_len % MOBA_BLOCK == 0
    nb = s_len // MOBA_BLOCK
    nbp = -(-nb // SUBLANES) * SUBLANES
    slots = LANES - hd
    assert nbp <= slots
    tq = MOBA_BLOCK
    return pl.pallas_call(
        functools.partial(_prompt_attn_kernel, n_heads=n_heads, hd=hd),
        grid=(b, s_len // tq),
        in_specs=[pl.BlockSpec((1, tq, aw), lambda bi, i: (bi, i, 0)),
                  pl.BlockSpec((1, s_len, aw), lambda bi, i: (bi, 0, 0)),
                  pl.BlockSpec((1, nb, aw, MOBA_BLOCK), lambda bi, i: (bi, 0, 0, 0))],
        out_specs=pl.BlockSpec((1, tq, aw), lambda bi, i: (bi, i, 0)),
        out_shape=jax.ShapeDtypeStruct((b, s_len, aw), BF16),
        scratch_shapes=[pltpu.VMEM((n_heads, s_len, hd + slots), BF16),
                        pltpu.VMEM((n_heads, nbp, hd), F32),
                        pltpu.VMEM((n_heads, hd + slots, tq), BF16)],
        compiler_params=_cparams(("arbitrary", "arbitrary")),
        name="prompt_attention",
    )(q_bf, k_bf, vtb)


def _head_diag(o, n_heads, hd, rows_per_head):
    row_head = lax.broadcasted_iota(jnp.int32, (o.shape[0], hd), 0) // rows_per_head
    out = jnp.zeros((o.shape[0], hd), F32)
    for h in range(n_heads):
        out = out + jnp.where(row_head == h, o[:, h * hd:(h + 1) * hd], 0.0)
    return out


def _sample_attn_kernel(pt_ref, q_ref, kn_ref, vn_ref, *rest, n_heads, hd, scale):
    del pt_ref
    npg = PAGES_PER_STEP
    k_pages = rest[:npg]
    v_pages = rest[npg:2 * npg]
    o_ref = rest[2 * npg]
    m_ref, l_ref, ksum_ref, acc_ref = rest[2 * npg + 1:]
    step = pl.program_id(1)
    n_steps = pl.num_programs(1)
    t_new = q_ref.shape[1]
    aw = n_heads * hd
    rows = n_heads * t_new
    page = k_pages[0].shape[4]
    pages_per_blk = MOBA_BLOCK // page
    n_past = acc_ref.shape[0]

    @pl.when(step == 0)
    def _init():
        m_ref[...] = jnp.zeros(m_ref.shape, F32)
        l_ref[...] = jnp.zeros(l_ref.shape, F32)
        ksum_ref[...] = jnp.zeros(ksum_ref.shape, F32)

    q_rep = jnp.concatenate([q_ref[0]] * n_heads, axis=0)
    own_head = (lax.broadcasted_iota(jnp.int32, (rows, aw), 0) // t_new
                == lax.broadcasted_iota(jnp.int32, (rows, aw), 1) // hd)
    q_bd = jnp.where(own_head, q_rep, 0.0)
    q_bd_bf = (q_bd * scale).astype(BF16)
    blk_lane = lax.broadcasted_iota(jnp.int32, (1, LANES), 1)
    ones_bf = jnp.ones((MOBA_BLOCK, LANES), BF16)

    n_blk = npg // pages_per_blk
    scored = []
    for bi in range(n_blk):
        pr = range(bi * pages_per_blk, (bi + 1) * pages_per_blk)
        kt_bf = jnp.concatenate([k_pages[r][0, 0].reshape(aw, page) for r in pr], axis=1).astype(BF16)
        scored.append((jnp.dot(q_bd_bf, kt_bf, preferred_element_type=F32),
                       jnp.dot(kt_bf, ones_bf, preferred_element_type=F32)))
    soft = []
    for s, _ in scored:
        m = jnp.max(s, axis=-1, keepdims=True)
        p = jnp.exp(s - m)
        soft.append((m, jnp.sum(p, axis=-1, keepdims=True), p.astype(BF16)))
    for bi in range(n_blk):
        pr = range(bi * pages_per_blk, (bi + 1) * pages_per_blk)
        vt_bf = jnp.concatenate([v_pages[r][0, 0].reshape(aw, page) for r in pr], axis=1).astype(BF16)
        m, l, p = soft[bi]
        o = _dot_nt(p, vt_bf)
        j = step * n_blk + bi
        here = blk_lane == j
        m_ref[...] = jnp.where(here, m, m_ref[...])
        l_ref[...] = jnp.where(here, l, l_ref[...])
        ksum_ref[...] = jnp.where(here, scored[bi][1], ksum_ref[...])
        acc_ref[j] = _head_diag(o, n_heads, hd, t_new)

    @pl.when(step == n_steps - 1)
    def _finish():
        s = _dot_nt(q_bd * scale, kn_ref[0])
        causal = (lax.broadcasted_iota(jnp.int32, (rows, t_new), 1)
                  <= lax.broadcasted_iota(jnp.int32, (rows, t_new), 0) % t_new)
        s = jnp.where(causal, s, -jnp.inf)
        m_own = jnp.max(s, axis=-1, keepdims=True)
        p = jnp.exp(s - m_own)
        l_own = jnp.sum(p, axis=-1, keepdims=True)
        o_own = _head_diag(jnp.dot(p, vn_ref[0], preferred_element_type=F32), n_heads, hd, t_new)

        gate = jnp.dot(q_bd, ksum_ref[...] * (1.0 / MOBA_BLOCK), preferred_element_type=F32, precision=HIGHEST)
        sel = _moba_select(gate, n_past, n_past, axis=1)
        m_all = m_ref[...]
        m_sel = jnp.maximum(m_own, jnp.max(jnp.where(sel, m_all, -jnp.inf), axis=-1, keepdims=True))
        w = jnp.where(sel, jnp.exp(m_all - m_sel), 0.0)
        w_own = jnp.exp(m_own - m_sel)
        den = l_own * w_own + jnp.sum(w * l_ref[...], axis=-1, keepdims=True)
        num = o_own * w_own
        for j in range(n_past):
            num = num + w[:, j:j + 1] * acc_ref[j]
        out = num / den
        o_ref[0] = jnp.concatenate([out[h * t_new:(h + 1) * t_new] for h in range(n_heads)], axis=1)


def _sample_attention(q, k_new, v_new, cache_kt, cache_vt, page_table, layer):
    n, t_new, aw = q.shape
    _, _, n_heads, hd, page = cache_kt.shape
    n_pages = page_table.shape[1]
    npg = PAGES_PER_STEP
    assert n_pages % npg == 0 and MOBA_BLOCK % page == 0 and npg % (MOBA_BLOCK // page) == 0
    n_past = n_pages * page // MOBA_BLOCK
    assert n_past <= LANES
    rows = n_heads * t_new

    def page_spec(r):
        return pl.BlockSpec((1, 1, n_heads, hd, page),
                            lambda ni, s, pt: (layer, pt[ni, s * npg + r], 0, 0, 0))

    new_spec = pl.BlockSpec((1, t_new, aw), lambda ni, s, pt: (ni, 0, 0))
    grid_spec = pltpu.PrefetchScalarGridSpec(
        num_scalar_prefetch=1,
        grid=(n, n_pages // npg),
        in_specs=[new_spec, new_spec, new_spec] + [page_spec(r) for r in range(npg)] * 2,
        out_specs=new_spec,
        scratch_shapes=[pltpu.VMEM((rows, LANES), F32),
                        pltpu.VMEM((rows, LANES), F32),
                        pltpu.VMEM((aw, LANES), F32),
                        pltpu.VMEM((n_past, rows, hd), F32)],
    )
    return pl.pallas_call(
        functools.partial(_sample_attn_kernel, n_heads=n_heads, hd=hd, scale=hd ** -0.5),
        grid_spec=grid_spec,
        out_shape=jax.ShapeDtypeStruct(q.shape, F32),
        compiler_params=_cparams(("arbitrary", "arbitrary")),
        name="sample_attention",
    )(page_table, q, k_new, v_new, *([cache_kt] * npg), *([cache_vt] * npg))


def _split_bf16(x):
    hi = x.astype(BF16)
    return hi, (x - hi.astype(F32)).astype(BF16)


def _post_kernel(att_ref, pz_ref, pre_ref, gab_ref, x_ref, gt1_ref, sh2_ref, sc2_ref,
                 wao_ref, wpool_ref, pscale_ref, wout_ref, gpost1_ref, gpre2_ref, wrh_ref, wrl_ref, br_ref,
                 x1_ref, h2_ref, topi_ref, gates_ref, *, pos0, zero_first_prefix):
    bb, tt, d = x_ref.shape
    rows = bb * tt
    i = pl.program_id(1)
    n_groups = len(POOL_WINDOWS)
    gc = pz_ref.shape[2] // n_groups

    pz = pz_ref[...]
    prefix = pre_ref[...]
    if zero_first_prefix:
        prefix = jnp.where(i == 0, 0.0, prefix)
    xp = jnp.concatenate([prefix, pz], axis=1)
    pos = pos0 + i * tt + lax.broadcasted_iota(jnp.int32, (1, tt, 1), 1)
    pbs = []
    for g, w in enumerate(POOL_WINDOWS):
        a = xp[:, :, g * gc:(g + 1) * gc]
        span = 1
        while span < w:
            a = a + pltpu.roll(a, span, 1)
            span *= 2
        win = a[:, POOL_PREFIX_ROWS:, :]
        cnt = jnp.minimum(w, pos + 1).astype(F32)
        pooled = win / cnt - pz[:, :, g * gc:(g + 1) * gc]
        pbs.append(jnp.dot(pooled.reshape(rows, gc).astype(BF16), wpool_ref[g], preferred_element_type=F32))
    pb = jnp.concatenate(pbs, axis=1) * pscale_ref[...]

    att = jnp.dot(att_ref[...].reshape(rows, att_ref.shape[2]).astype(BF16), wao_ref[...],
                  preferred_element_type=F32)
    gab = gab_ref[...].reshape(rows, 2 * d).astype(F32)
    mix = _sigmoid(gab[:, :d]) * att + _sigmoid(gab[:, d:]) * pb
    mix = jnp.dot(mix.astype(BF16), wout_ref[...], preferred_element_type=F32)
    x1 = x_ref[...] + gt1_ref[...] * _rms(mix, gpost1_ref[...]).reshape(bb, tt, d)
    x1_ref[...] = x1
    h2 = (_rms(x1, gpre2_ref[...]) * (1.0 + sc2_ref[...]) + sh2_ref[...]).reshape(rows, d)
    _store_tile_rows(h2_ref, h2)

    h_hi, h_lo = _split_bf16(h2)
    logits = (jnp.dot(h_hi, wrh_ref[...], preferred_element_type=F32)
              + (jnp.dot(h_lo, wrh_ref[...], preferred_element_type=F32)
                 + jnp.dot(h_hi, wrl_ref[...], preferred_element_type=F32))) + br_ref[...]
    n_exp = logits.shape[1]
    lane = lax.broadcasted_iota(jnp.int32, logits.shape, 1).astype(F32)
    kcol = lax.broadcasted_iota(jnp.int32, (rows, MOE_TOP_K), 1)
    vals = jnp.zeros((rows, MOE_TOP_K), F32)
    idxs = jnp.zeros((rows, MOE_TOP_K), F32)
    v = logits
    for k in range(MOE_TOP_K):
        mx = jnp.max(v, axis=-1, keepdims=True)
        ix = jnp.min(jnp.where(v == mx, lane, float(n_exp)), axis=-1, keepdims=True)
        vals = jnp.where(kcol == k, mx, vals)
        idxs = jnp.where(kcol == k, ix, idxs)
        v = jnp.where(lane == ix, -jnp.inf, v)
    e = jnp.exp(vals - vals[:, 0:1])
    gates = e / jnp.sum(e, axis=-1, keepdims=True)
    topi_ref[...] = idxs.astype(jnp.int32).reshape(bb, tt, MOE_TOP_K)
    gates_ref[...] = gates.reshape(bb, tt, MOE_TOP_K)


def _post(att, pz, prefix, gab, x, mod3, weights, bb, tt, pos0, zero_first_prefix):
    bx, t, d = x.shape
    aw, pw = att.shape[2], pz.shape[2]
    grid = (bx // bb, t // tt)
    row = lambda width: pl.BlockSpec((bb, tt, width), lambda b, i: (b, i, 0))
    modspec = lambda slot: pl.BlockSpec((bb, 1, d), lambda b, i: (b, 0, slot))
    full = lambda a: pl.BlockSpec(a.shape, lambda b, i: (0,) * a.ndim)
    if zero_first_prefix:
        per_tile = tt // POOL_PREFIX_ROWS
        pre_spec = pl.BlockSpec((bb, POOL_PREFIX_ROWS, pw), lambda b, i: (b, jnp.maximum(i * per_tile - 1, 0), 0))
    else:
        assert t == tt
        pre_spec = pl.BlockSpec((bb, POOL_PREFIX_ROWS, pw), lambda b, i: (b, 0, 0))
    out = lambda width, dt: jax.ShapeDtypeStruct((bx, t, width), dt)
    return pl.pallas_call(
        functools.partial(_post_kernel, pos0=pos0, zero_first_prefix=zero_first_prefix),
        grid=grid,
        in_specs=[row(aw), row(pw), pre_spec, row(2 * d), row(d), modspec(2), modspec(3), modspec(4)]
                 + [full(w) for w in weights],
        out_specs=[row(d), pl.BlockSpec((bb * tt * SUBLANES, LANES), lambda b, i: (b * grid[1] + i, 0)),
                   row(MOE_TOP_K), row(MOE_TOP_K)],
        out_shape=[out(d, F32), jax.ShapeDtypeStruct((bx * t * SUBLANES, LANES), F32),
                   out(MOE_TOP_K, jnp.int32), out(MOE_TOP_K, F32)],
        compiler_params=_cparams(("arbitrary", "arbitrary")),
        name="post",
    )(att, pz, prefix, gab, x, mod3, mod3, mod3, *weights)


def _plan_kernel(topi_ref, rank_ref, counts_ref, base_ref):
    i = pl.program_id(0)
    tt = topi_ref.shape[0]
    n_exp = counts_ref.shape[1]

    @pl.when(i == 0)
    def _init():
        base_ref[...] = jnp.zeros(base_ref.shape, F32)

    topi = topi_ref[...]
    lane = lax.broadcasted_iota(jnp.int32, (tt, n_exp), 1)
    onehots = [jnp.where(topi[:, k:k + 1] == lane, 1.0, 0.0) for k in range(MOE_TOP_K)]
    tot = onehots[0]
    for oh in onehots[1:]:
        tot = tot + oh
    tri = jnp.where(lax.broadcasted_iota(jnp.int32, (tt, tt), 0) > lax.broadcasted_iota(jnp.int32, (tt, tt), 1),
                    1.0, 0.0).astype(BF16)
    before = jnp.dot(tri, tot.astype(BF16), preferred_element_type=F32) + base_ref[...]
    kcol = lax.broadcasted_iota(jnp.int32, (tt, MOE_TOP_K), 1)
    rank = jnp.zeros((tt, MOE_TOP_K), F32)
    for k in range(MOE_TOP_K):
        rank = jnp.where(kcol == k, jnp.sum(onehots[k] * before, axis=-1, keepdims=True), rank)
    rank_ref[...] = rank.astype(jnp.int32)
    base_ref[...] = base_ref[...] + jnp.sum(tot, axis=0, keepdims=True)
    counts_ref[...] = base_ref[...].astype(jnp.int32)


def _moe_plan(topi, n_exp):
    t = topi.shape[0]
    tt = PLAN_ROWS
    assert t % tt == 0
    return pl.pallas_call(
        _plan_kernel,
        grid=(t // tt,),
        in_specs=[pl.BlockSpec((tt, MOE_TOP_K), lambda i: (i, 0))],
        out_specs=[pl.BlockSpec((tt, MOE_TOP_K), lambda i: (i, 0)),
                   pl.BlockSpec((1, n_exp), lambda i: (0, 0))],
        out_shape=[jax.ShapeDtypeStruct((t, MOE_TOP_K), jnp.int32),
                   jax.ShapeDtypeStruct((1, n_exp), jnp.int32)],
        scratch_shapes=[pltpu.VMEM((1, n_exp), F32)],
        compiler_params=_cparams(("arbitrary",)),
        name="moe_plan",
    )(topi)


def _dispatch_kernel(last_blk_ref, dest_ref, ha_ref, hb_ref, xs_ref, zero_ref, sem, zsem, *, steps_a):
    i = pl.program_id(0)
    tile_rows = ha_ref.shape[0]
    tt = tile_rows // SUBLANES
    n_fill = last_blk_ref.shape[0]

    @pl.when(i == 0)
    def _zero_padding():
        zero_ref[...] = jnp.zeros(zero_ref.shape, F32)

        def fill(e):
            start = pl.multiple_of(jnp.maximum(last_blk_ref[e], 0), SUBLANES)
            return pltpu.make_async_copy(zero_ref, xs_ref.at[pl.ds(start, zero_ref.shape[0]), :], zsem)

        for e in range(n_fill):
            @pl.when(last_blk_ref[e] >= 0)
            def _start(e=e):
                fill(e).start()
        for e in range(n_fill):
            @pl.when(last_blk_ref[e] >= 0)
            def _wait(e=e):
                fill(e).wait()

    def scatter_rows(h_ref):
        def row_copy(r, k):
            dst = pl.multiple_of(dest_ref[0, 0, r * MOE_TOP_K + k], SUBLANES)
            return pltpu.make_async_copy(h_ref.at[pl.ds(pl.multiple_of(r * SUBLANES, SUBLANES), SUBLANES), :],
                                         xs_ref.at[pl.ds(dst, SUBLANES), :], sem)

        def issue(r, carry):
            for k in range(MOE_TOP_K):
                row_copy(r, k).start(priority=k % DMA_QUEUES)
            return carry

        lax.fori_loop(0, tt, issue, 0, unroll=ISSUE_UNROLL)
        for k in range(MOE_TOP_K):
            pltpu.make_async_copy(h_ref, xs_ref.at[pl.ds(0, tile_rows), :], sem).wait()

    @pl.when(i < steps_a)
    def _group_a():
        scatter_rows(ha_ref)

    @pl.when(i >= steps_a)
    def _group_b():
        scatter_rows(hb_ref)


def _moe_dispatch(last_blk, dest3, h3_a, h3_b, p_rows, bm):
    tile_rows = DISPATCH_ROWS * SUBLANES
    assert h3_a.shape[0] % tile_rows == 0 and h3_b.shape[0] % tile_rows == 0
    steps_a, steps_b = h3_a.shape[0] // tile_rows, h3_b.shape[0] // tile_rows
    grid_spec = pltpu.PrefetchScalarGridSpec(
        num_scalar_prefetch=1,
        grid=(steps_a + steps_b,),
        in_specs=[pl.BlockSpec((1, 1, DISPATCH_ROWS * MOE_TOP_K), lambda i, lb: (i, 0, 0), memory_space=pltpu.SMEM),
                  pl.BlockSpec((tile_rows, LANES), lambda i, lb: (jnp.minimum(i, steps_a - 1), 0)),
                  pl.BlockSpec((tile_rows, LANES), lambda i, lb: (jnp.maximum(i - steps_a, 0), 0))],
        out_specs=pl.BlockSpec(memory_space=pl.ANY),
        scratch_shapes=[pltpu.VMEM((bm * SUBLANES, LANES), F32),
                        pltpu.SemaphoreType.DMA(()), pltpu.SemaphoreType.DMA(())],
    )
    return pl.pallas_call(
        functools.partial(_dispatch_kernel, steps_a=steps_a),
        grid_spec=grid_spec,
        out_shape=jax.ShapeDtypeStruct((p_rows * SUBLANES, LANES), F32),
        compiler_params=_cparams(("arbitrary",)),
        name="moe_dispatch",
    )(last_blk, dest3, h3_a, h3_b)


def _expert_kernel(blk_e_ref, blk_first_ref, n_used_ref, xs_ref, wgu_ref, bgu_ref, wd_ref, bd_ref, ys_ref,
                   wgu_bf_ref, wd_bf_ref):
    del blk_e_ref
    i = pl.program_id(0)
    dff = wd_ref.shape[1]
    bm = xs_ref.shape[0] // SUBLANES
    used = i < n_used_ref[0]

    @pl.when(used & (blk_first_ref[i] == 1))
    def _cast_weights():
        wgu_bf_ref[...] = wgu_ref[0].astype(BF16)
        wd_bf_ref[...] = wd_ref[0].astype(BF16)

    @pl.when(used)
    def _compute():
        x = _load_tile_rows(xs_ref, bm).astype(BF16)
        gu = jnp.dot(x, wgu_bf_ref[...], preferred_element_type=F32) + bgu_ref[0]
        gt = jnp.minimum(gu[:, :dff], SWIGLU_LIMIT)
        up = jnp.clip(gu[:, dff:], -SWIGLU_LIMIT, SWIGLU_LIMIT)
        act = (up + 1.0) * (gt * _sigmoid(gt * SWIGLU_ALPHA))
        _store_tile_rows(ys_ref, jnp.dot(act.astype(BF16), wd_bf_ref[...], preferred_element_type=F32) + bd_ref[0])

    @pl.when(jnp.logical_not(used))
    def _skip():
        ys_ref[...] = jnp.zeros(ys_ref.shape, F32)


def _moe_experts(blk_e, blk_first, n_used, xs3, w_gu, b_gu3, w_down, b_down3):
    bm = EXPERT_ROWS
    n_exp, d, two_f = w_gu.shape
    dff = w_down.shape[1]
    assert d == SUBLANES * LANES
    tile_rows = bm * SUBLANES
    grid_spec = pltpu.PrefetchScalarGridSpec(
        num_scalar_prefetch=3,
        grid=(xs3.shape[0] // tile_rows,),
        in_specs=[pl.BlockSpec((tile_rows, LANES), lambda i, be, bf, nu: (i, 0)),
                  pl.BlockSpec((1, d, two_f), lambda i, be, bf, nu: (be[i], 0, 0)),
                  pl.BlockSpec((1, 1, two_f), lambda i, be, bf, nu: (be[i], 0, 0)),
                  pl.BlockSpec((1, dff, d), lambda i, be, bf, nu: (be[i], 0, 0)),
                  pl.BlockSpec((1, 1, d), lambda i, be, bf, nu: (be[i], 0, 0))],
        out_specs=pl.BlockSpec((tile_rows, LANES), lambda i, be, bf, nu: (i, 0)),
        scratch_shapes=[pltpu.VMEM((d, two_f), BF16), pltpu.VMEM((dff, d), BF16)],
    )
    return pl.pallas_call(
        _expert_kernel,
        grid_spec=grid_spec,
        out_shape=jax.ShapeDtypeStruct(xs3.shape, F32),
        compiler_params=_cparams(("arbitrary",)),
        name="moe_experts",
    )(blk_e, blk_first, n_used, xs3, w_gu, b_gu3, w_down, b_down3)


def _combine_kernel(dest_ref, dest_next_ref, gates_ref, x1_ref, gt2_ref, gpost2_ref, ys_ref, y_ref, buf_ref, sems):
    bb, tt, d = x1_ref.shape
    rows = bb * tt
    step = pl.program_id(0) * pl.num_programs(1) + pl.program_id(1)
    n_steps = pl.num_programs(0) * pl.num_programs(1)
    slot = step % 2

    def gather_rows(idx_ref, slot):
        def row_copy(r, k):
            src = pl.multiple_of(idx_ref[0, 0, r * MOE_TOP_K + k], SUBLANES)
            dst = buf_ref.at[slot, k, pl.ds(pl.multiple_of(r * SUBLANES, SUBLANES), SUBLANES), :]
            return pltpu.make_async_copy(ys_ref.at[pl.ds(src, SUBLANES), :], dst, sems.at[slot])

        def issue(r, carry):
            for k in range(MOE_TOP_K):
                row_copy(r, k).start(priority=k % DMA_QUEUES)
            return carry

        lax.fori_loop(0, rows, issue, 0, unroll=ISSUE_UNROLL)

    @pl.when(step == 0)
    def _first():
        gather_rows(dest_ref, slot)

    @pl.when(step + 1 < n_steps)
    def _prefetch():
        gather_rows(dest_next_ref, 1 - slot)

    for k in range(MOE_TOP_K):
        pltpu.make_async_copy(ys_ref.at[pl.ds(0, rows * SUBLANES), :], buf_ref.at[slot, k], sems.at[slot]).wait()

    gates = gates_ref[...].reshape(rows, MOE_TOP_K)
    ff = gates[:, 0:1] * _load_tile_rows(buf_ref, rows, lead=(slot, 0))
    for k in range(1, MOE_TOP_K):
        ff = ff + gates[:, k:k + 1] * _load_tile_rows(buf_ref, rows, lead=(slot, k))
    y_ref[...] = x1_ref[...] + gt2_ref[...] * _rms(ff, gpost2_ref[...]).reshape(bb, tt, d)


def _moe_combine(dest3, gates, x1, mod3, g_post2, ys, bb, tt):
    bx, t, d = x1.shape
    rows = bb * tt
    n_t = t // tt
    row = lambda width: pl.BlockSpec((bb, tt, width), lambda b, i: (b, i, 0))
    last_step = (bx // bb) * n_t - 1
    idx_spec = lambda ahead: pl.BlockSpec((1, 1, rows * MOE_TOP_K),
                                          lambda b, i: (jnp.minimum(b * n_t + i + ahead, last_step), 0, 0),
                                          memory_space=pltpu.SMEM)
    return pl.pallas_call(
        _combine_kernel,
        grid=(bx // bb, n_t),
        in_specs=[idx_spec(0), idx_spec(1),
                  row(MOE_TOP_K), row(d),
                  pl.BlockSpec((bb, 1, d), lambda b, i: (b, 0, 5)),
                  pl.BlockSpec((1, d), lambda b, i: (0, 0)),
                  pl.BlockSpec(memory_space=pl.ANY)],
        out_specs=row(d),
        out_shape=jax.ShapeDtypeStruct((bx, t, d), F32),
        scratch_shapes=[pltpu.VMEM((2, MOE_TOP_K, rows * SUBLANES, LANES), F32), pltpu.SemaphoreType.DMA((2,))],
        compiler_params=_cparams(("arbitrary", "arbitrary")),
        name="moe_combine",
    )(dest3, dest3, gates, x1, mod3, g_post2, ys)


def _layer(xp, xs, cp, cs, cache_kt, cache_vt, state_pool_l, page_table, layer, p):
    b, s_len, d = xp.shape
    n, t_new, _ = xs.shape
    n_heads, hd, page = cache_kt.shape[2:]
    aw = n_heads * hd
    pw = state_pool_l.shape[2]
    n_exp = p['w_router'].shape[1]
    sb = min(SAMPLE_SEQS, n)
    pr = min(PROMPT_ROWS, s_len)

    mod = _mod(jnp.concatenate([cp, cs], axis=0), p['w_mod'], p['b_mod'])
    mod_p = mod[:b].reshape(b, 1, N_MOD * d)
    mod_s = mod[b:].reshape(n, 1, N_MOD * d)

    w_in_bf = p['w_in'].astype(BF16)
    w_kvt_bf = p['w_in'][:, aw:3 * aw].T.astype(BF16)
    qp, kp, ktp, vtp, vtbp, pzp, gabp = _in_proj_prompt(xp, mod_p, p['g_pre1'], w_in_bf, w_kvt_bf, aw, pw, pr,
                                                        hd ** -0.5 * LOG2_E)
    qs, ks, vs, pzs, gabs = _in_proj_sample(xs, mod_s, p['g_pre1'], w_in_bf, aw, pw, sb)

    att_p = _prompt_attention(qp, kp, vtbp, n_heads, hd)
    att_s = _sample_attention(qs, ks, vs, cache_kt, cache_vt, page_table, layer)

    wr_hi = p['w_router'].astype(BF16)
    wr_lo = (p['w_router'] - wr_hi.astype(F32)).astype(BF16)
    post_w = (p['w_attn_o'].astype(BF16), p['w_pool'].astype(BF16), p['pool_scale'], p['w_out'].astype(BF16),
              p['g_post1'], p['g_pre2'], wr_hi, wr_lo, p['b_router'])
    x1p, h2p, tip, gp = _post(att_p, pzp, pzp, gabp, xp, mod_p, post_w, 1, pr, 0, True)
    state16 = jnp.concatenate([jnp.zeros((n, 1, pw), F32), state_pool_l], axis=1)
    x1s, h2s, tis, gs = _post(att_s, pzs, state16, gabs, xs, mod_s, post_w, sb, t_new,
                              page_table.shape[1] * page, False)

    tp, ts = b * s_len, n * t_new
    topi = jnp.concatenate([tip.reshape(tp, MOE_TOP_K), tis.reshape(ts, MOE_TOP_K)], axis=0)
    t_all = tp + ts
    rank, counts = _moe_plan(topi, n_exp)
    bm = EXPERT_ROWS
    counts = counts[0]
    padded = (counts + bm - 1) // bm * bm
    pends = jnp.cumsum(padded)
    pstarts = pends - padded
    n_blocks = (t_all * MOE_TOP_K + n_exp * (bm - 1) + bm - 1) // bm
    blk_start = jnp.arange(n_blocks, dtype=jnp.int32) * bm
    blk_e = jnp.minimum(jnp.sum((pends[None, :] <= blk_start[:, None]).astype(jnp.int32), axis=1), n_exp - 1)
    blk_first = jnp.concatenate([jnp.ones((1,), jnp.int32), (blk_e[1:] != blk_e[:-1]).astype(jnp.int32)])
    n_used = (pends[-1:] // bm).astype(jnp.int32)
    tail_blk = n_used + jnp.arange(n_exp, dtype=jnp.int32)
    last_blk = jnp.concatenate([jnp.where(padded > 0, (pends - bm) * SUBLANES, -1),
                                jnp.where(tail_blk < n_blocks, tail_blk * (bm * SUBLANES), -1)]).astype(jnp.int32)
    onehot_e = topi[:, :, None] == jnp.arange(n_exp, dtype=jnp.int32)
    dest = (jnp.sum(jnp.where(onehot_e, pstarts.astype(jnp.int32), 0), axis=-1) + rank) * SUBLANES

    xs_sorted = _moe_dispatch(last_blk, dest.reshape(t_all // DISPATCH_ROWS, 1, DISPATCH_ROWS * MOE_TOP_K),
                              h2p, h2s, n_blocks * bm, bm)
    ys = _moe_experts(blk_e, blk_first, n_used, xs_sorted, p['w_gu'], p['b_gu'][:, None, :],
                      p['w_down'], p['b_down'][:, None, :])

    cr = min(COMBINE_ROWS, s_len)
    csb = min(max(COMBINE_ROWS // t_new, 1), n)
    dest_p = dest[:tp].reshape(tp // cr, 1, cr * MOE_TOP_K)
    dest_s = dest[tp:].reshape(ts // (csb * t_new), 1, csb * t_new * MOE_TOP_K)
    yp = _moe_combine(dest_p, gp, x1p, mod_p, p['g_post2'], ys, 1, cr)
    ysm = _moe_combine(dest_s, gs, x1s, mod_s, p['g_post2'], ys, csb, t_new)

    k_out = ktp.reshape(b, n_heads, hd, s_len).transpose(0, 3, 1, 2)
    v_out = vtp.reshape(b, n_heads, hd, s_len).transpose(0, 3, 1, 2)
    pool_p = pzp[:, s_len - (POOL_PREFIX_ROWS - 1):, :]
    pool_s = jnp.concatenate([state_pool_l, pzs], axis=1)[:, -(POOL_PREFIX_ROWS - 1):, :]
    return (yp, ysm, k_out, v_out, pool_p,
            ks.reshape(n, t_new, n_heads, hd), vs.reshape(n, t_new, n_heads, hd), pool_s)


def kernel(x_prompt, x_sample, c_prompt, c_sample, cache_k, cache_v, state_pool, page_table, w_mod, b_mod, g_pre1, g_post1, g_pre2, g_post2, w_in, w_attn_o, w_pool, pool_scale, w_out, w_router, b_router, w_gu, b_gu, w_down, b_down):
    depth = w_mod.shape[0]
    hp, hs = x_prompt, x_sample
    cache_kt = cache_k.transpose(0, 1, 3, 4, 2)
    cache_vt = cache_v.transpose(0, 1, 3, 4, 2)
    outs = [[] for _ in range(6)]
    for l in range(depth):
        p = {'w_mod': w_mod[l], 'b_mod': b_mod[l][None, :], 'g_pre1': g_pre1[l][None, :],
             'g_post1': g_post1[l][None, :], 'g_pre2': g_pre2[l][None, :], 'g_post2': g_post2[l][None, :],
             'w_in': w_in[l], 'w_attn_o': w_attn_o[l], 'w_pool': w_pool[l], 'pool_scale': pool_scale[l][None, :],
             'w_out': w_out[l], 'w_router': w_router[l], 'b_router': b_router[l][None, :],
             'w_gu': w_gu[l], 'b_gu': b_gu[l], 'w_down': w_down[l], 'b_down': b_down[l]}
        hp, hs, kp, vp, pp, ks, vs, ps = _layer(hp, hs, c_prompt, c_sample, cache_kt, cache_vt, state_pool[l],
                                                page_table, l, p)
        for lst, val in zip(outs, (kp, vp, pp, ks, vs, ps)):
            lst.append(val)
    return (hp, hs) + tuple(jnp.stack(o) for o in outs)
```
